```python
import jax, jax.numpy as jnp
from jax import lax
import numpy as np

D_MODEL = 1024
BATCH = 32
SEQ = 2048
DEPTH = 2

GRID_W = 64
CTX_LEN = 256
HEAD_DIM = 64
A_HEADS = 6
A_KV_HEADS = 2
A_GROUP = A_HEADS // A_KV_HEADS
B_HEADS = 5
NA_WIN_R = 8
NA_WIN_C = 16
C_HEADS = 5
C_Q_RANK = 256
C_KV_RANK = 128
C_NOPE = 64
C_ROPE = 32
C_V = 64
MIX_WIDTH = (A_HEADS + B_HEADS) * HEAD_DIM + C_HEADS * C_V
P_A = (A_HEADS + 2 * A_KV_HEADS) * HEAD_DIM
P_B = 3 * B_HEADS * HEAD_DIM
P_C = C_Q_RANK + C_KV_RANK + C_ROPE
P_IN = P_A + P_B + P_C
Q_BLOCK = 128
ROPE_THETA = 10000.0
N_GROUPS = 4
EXPERTS_PER_GROUP = 8
N_EXPERTS = N_GROUPS * EXPERTS_PER_GROUP
TOP_K = 2
D_EXPERT = 256
ALPHA = (2 * DEPTH) ** 0.25
BETA = (8 * DEPTH) ** -0.25
EPS = 1e-6
N_MOD = 6

kernel_name = "hybrid_dit_gqa_na_mla_hmoe"


def _ln(x):
    xf = x.astype(jnp.float32)
    mu = jnp.mean(xf, -1, keepdims=True)
    var = jnp.mean(jnp.square(xf - mu), -1, keepdims=True)
    return ((xf - mu) * lax.rsqrt(var + EPS)).astype(x.dtype)


def _ln_affine(x, g, b):
    return _ln(x) * g + b


def _rms(x, g):
    xf = x.astype(jnp.float32)
    y = xf * lax.rsqrt(jnp.mean(jnp.square(xf), -1, keepdims=True) + EPS)
    return y.astype(x.dtype) * g


def _axial_rope(n_tok, dim):
    t = jnp.arange(n_tok, dtype=jnp.int32)
    row = (t // GRID_W).astype(jnp.float32)
    col = (t % GRID_W).astype(jnp.float32)
    n_freq = dim // 4
    inv = ROPE_THETA ** (-jnp.arange(n_freq, dtype=jnp.float32) / n_freq)
    ang = jnp.concatenate([row[:, None] * inv, col[:, None] * inv], -1)
    return jnp.cos(ang), jnp.sin(ang)


def _apply_rope(x, cos, sin):
    shape = (1, x.shape[1]) + (1,) * (x.ndim - 3) + (cos.shape[-1],)
    cos = cos.reshape(shape)
    sin = sin.reshape(shape)
    x1, x2 = jnp.split(x.astype(jnp.float32), 2, -1)
    return jnp.concatenate([x1 * cos - x2 * sin, x1 * sin + x2 * cos], -1).astype(x.dtype)


def _latent_attention(q, k, v, kc, vc, scale):
    B, S = q.shape[:2]
    nb = S // Q_BLOCK
    k_all = jnp.concatenate([kc, k], 1)
    v_all = jnp.concatenate([vc, v], 1)
    qb = q.reshape((B, nb, Q_BLOCK) + q.shape[2:]).swapaxes(0, 1)

    def one_block(q_blk):
        s = jnp.einsum('bqkgd,bskd->bkgqs', q_blk, k_all).astype(jnp.float32) * scale
        prob = jax.nn.softmax(s, -1).astype(v.dtype)
        return jnp.einsum('bkgqs,bskd->bqkgd', prob, v_all)

    out = lax.map(one_block, qb)
    return out.swapaxes(0, 1).reshape(B, S, -1)


def _context_attention(q, k, v, scale):
    B, L = q.shape[:2]
    s = jnp.einsum('blkgd,bmkd->bkglm', q, k).astype(jnp.float32) * scale
    prob = jax.nn.softmax(s, -1).astype(v.dtype)
    return jnp.einsum('bkglm,bmkd->blkgd', prob, v).reshape(B, L, -1)


def _neighbourhood_attention(q, k, v, kc, vc, rpb, scale):
    B, S, H, d = q.shape
    rows = S // GRID_W
    win_r = min(NA_WIN_R, rows)
    qg = q.reshape(B, rows, GRID_W, H, d)
    kg = k.reshape(B, rows, GRID_W, H, d)
    vg = v.reshape(B, rows, GRID_W, H, d)
    col = jnp.arange(GRID_W, dtype=jnp.int32)
    c0 = jnp.clip(col - NA_WIN_C // 2, 0, GRID_W - NA_WIN_C)
    col_in = (col[None, :] >= c0[:, None]) & (col[None, :] < c0[:, None] + NA_WIN_C)
    dc = jnp.clip(col[None, :] - col[:, None] + (NA_WIN_C - 1), 0, 2 * NA_WIN_C - 2)
    rpb_c = rpb[:, :, dc]
    n_loc = win_r * GRID_W

    def one_row(r):
        r0 = jnp.clip(r - win_r // 2, 0, rows - win_r)
        q_r = lax.dynamic_index_in_dim(qg, r, axis=1, keepdims=False)
        k_r = lax.dynamic_slice_in_dim(kg, r0, win_r, axis=1)
        v_r = lax.dynamic_slice_in_dim(vg, r0, win_r, axis=1)
        dr = r0 + jnp.arange(win_r, dtype=jnp.int32) - r + (NA_WIN_R - 1)
        bias = jnp.take(rpb_c, dr, axis=1).transpose(0, 2, 1, 3)
        s_loc = jnp.einsum('bqhd,bikhd->bhqik', q_r, k_r).astype(jnp.float32) * scale + bias
        s_loc = jnp.where(col_in[:, None, :], s_loc, -jnp.inf)
        s_ctx = jnp.einsum('bqhd,blhd->bhql', q_r, kc).astype(jnp.float32) * scale
        s = jnp.concatenate([s_loc.reshape(B, H, GRID_W, n_loc), s_ctx], -1)
        prob = jax.nn.softmax(s, -1).astype(v.dtype)
        p_loc = prob[..., :n_loc].reshape(B, H, GRID_W, win_r, GRID_W)
        p_ctx = prob[..., n_loc:]
        return (jnp.einsum('bhqik,bikhd->bqhd', p_loc, v_r)
                + jnp.einsum('bhql,blhd->bqhd', p_ctx, vc))

    out = lax.map(one_row, jnp.arange(rows, dtype=jnp.int32))
    return out.transpose(1, 0, 2, 3, 4).reshape(B, S, H * d)


def _mixer_gqa(p, p_ctx, q_gain, k_gain, cos, sin, with_ctx):
    def heads(t):
        q, k, v = jnp.split(t, [A_HEADS * HEAD_DIM, (A_HEADS + A_KV_HEADS) * HEAD_DIM], -1)
        lead = t.shape[:2]
        q = _rms(q.reshape(lead + (A_KV_HEADS, A_GROUP, HEAD_DIM)), q_gain)
        k = _rms(k.reshape(lead + (A_KV_HEADS, HEAD_DIM)), k_gain)
        return q, k, v.reshape(lead + (A_KV_HEADS, HEAD_DIM))

    q, k, v = heads(p)
    qc, kc, vc = heads(p_ctx)
    q = _apply_rope(q, cos, sin)
    k = _apply_rope(k, cos, sin)
    scale = HEAD_DIM ** -0.5
    y = _latent_attention(q, k, v, kc, vc, scale)
    y_ctx = _context_attention(qc, kc, vc, scale) if with_ctx else None
    return y, y_ctx


def _mixer_na(p, p_ctx, rpb, with_ctx):
    def heads(t):
        lead = t.shape[:2]
        return [u.reshape(lead + (B_HEADS, HEAD_DIM)) for u in jnp.split(t, 3, -1)]

    q, k, v = heads(p)
    qc, kc, vc = heads(p_ctx)
    scale = HEAD_DIM ** -0.5
    y = _neighbourhood_attention(q, k, v, kc, vc, rpb, scale)
    y_ctx = _context_attention(qc[:, :, :, None, :], kc, vc, scale) if with_ctx else None
    return y, y_ctx


def _mixer_mla(p, p_ctx, q_lat_gain, kv_lat_gain, w_uq, w_ukv, cos, sin, with_ctx):
    def project(t):
        cq, ckv, kr = jnp.split(t, [C_Q_RANK, C_Q_RANK + C_KV_RANK], -1)
        lead = t.shape[:2]
        q = (_rms(cq, q_lat_gain) @ w_uq).reshape(lead + (C_HEADS, 1, C_NOPE + C_ROPE))
        kv = (_rms(ckv, kv_lat_gain) @ w_ukv).reshape(lead + (C_HEADS, C_NOPE + C_V))
        k_nope, v = jnp.split(kv, [C_NOPE], -1)
        return q, k_nope, kr[:, :, None, :], v

    def keys(k_nope, kr):
        return jnp.concatenate([k_nope, jnp.broadcast_to(kr, k_nope.shape[:-1] + (C_ROPE,))], -1)

    q, k_nope, kr, v = project(p)
    qc, kc_nope, krc, vc = project(p_ctx)
    q = jnp.concatenate([q[..., :C_NOPE], _apply_rope(q[..., C_NOPE:], cos, sin)], -1)
    k = keys(k_nope, _apply_rope(kr, cos, sin))
    kc = keys(kc_nope, krc)
    scale = (C_NOPE + C_ROPE) ** -0.5
    y = _latent_attention(q, k, v, kc, vc, scale)
    y_ctx = _context_attention(qc, kc, vc, scale) if with_ctx else None
    return y, y_ctx


def _hier_moe(h, w_rg, b_rg, w_re, b_re, w1, w3, w2):
    g_logit = (h @ w_rg).astype(jnp.float32) + b_rg
    g_prob = jax.nn.softmax(g_logit, -1)
    g_idx = jnp.argmax(g_logit, -1)
    g_w = jnp.take_along_axis(g_prob, g_idx[..., None], -1)
    e_logit = ((h @ w_re).astype(jnp.float32) + b_re).reshape(h.shape[:-1] + (N_GROUPS, EXPERTS_PER_GROUP))
    e_in_group = jnp.take_along_axis(e_logit, g_idx[..., None, None], -2)[..., 0, :]
    top_v, top_i = lax.top_k(e_in_group, TOP_K)
    top_w = jax.nn.softmax(top_v, -1) * g_w
    expert_id = g_idx[..., None] * EXPERTS_PER_GROUP + top_i
    combine = jnp.sum(jax.nn.one_hot(expert_id, N_EXPERTS, dtype=jnp.float32) * top_w[..., None], -2).astype(h.dtype)
    out = jnp.zeros_like(h)
    for e in range(N_EXPERTS):
        hidden = jax.nn.silu(h @ w1[e]) * (h @ w3[e])
        out = out + combine[..., e:e + 1] * (hidden @ w2[e])
    return out


def setup_inputs(seed: int = 0) -> dict:
    key = jax.random.key(seed)
    ks = iter(jax.random.split(key, 32))

    def nrm(shape, scale):
        return jax.random.normal(next(ks), shape, jnp.float32) * scale

    L, D = DEPTH, D_MODEL
    return {
        "x": nrm((BATCH, SEQ, D), 1.0),
        "c": nrm((BATCH, D), 1.0),
        "ctx": nrm((BATCH, CTX_LEN, D), 1.0),
        "c_ctx": nrm((D,), 1.0),
        "w_mod": nrm((L, D, N_MOD * D), 0.5 * D ** -0.5),
        "b_mod": nrm((L, N_MOD * D), 0.01),
        "w_in": nrm((L, D, P_IN), D ** -0.5),
        "q_gain_a": 1.0 + nrm((L, HEAD_DIM), 0.01),
        "k_gain_a": 1.0 + nrm((L, HEAD_DIM), 0.01),
        "rpb_b": nrm((L, B_HEADS, 2 * NA_WIN_R - 1, 2 * NA_WIN_C - 1), 0.1),
        "q_lat_gain": 1.0 + nrm((L, C_Q_RANK), 0.01),
        "kv_lat_gain": 1.0 + nrm((L, C_KV_RANK), 0.01),
        "w_uq": nrm((L, C_Q_RANK, C_HEADS * (C_NOPE + C_ROPE)), C_Q_RANK ** -0.5),
        "w_ukv": nrm((L, C_KV_RANK, C_HEADS * (C_NOPE + C_V)), C_KV_RANK ** -0.5),
        "w_out": nrm((L, MIX_WIDTH, D), BETA * MIX_WIDTH ** -0.5),
        "ln1_g": 1.0 + nrm((L, D), 0.01),
        "ln1_b": nrm((L, D), 0.01),
        "w_rg": nrm((L, D, N_GROUPS), D ** -0.5),
        "b_rg": nrm((L, N_GROUPS), 0.01),
        "w_re": nrm((L, D, N_EXPERTS), D ** -0.5),
        "b_re": nrm((L, N_EXPERTS), 0.01),
        "w1": nrm((L, N_EXPERTS, D, D_EXPERT), D ** -0.5),
        "w3": nrm((L, N_EXPERTS, D, D_EXPERT), D ** -0.5),
        "w2": nrm((L, N_EXPERTS, D_EXPERT, D), BETA * D_EXPERT ** -0.5),
        "ln2_g": 1.0 + nrm((L, D), 0.01),
        "ln2_b": nrm((L, D), 0.01),
    }


def reference(x, c, ctx, c_ctx, w_mod, b_mod, w_in, q_gain_a, k_gain_a, rpb_b, q_lat_gain, kv_lat_gain,
              w_uq, w_ukv, w_out, ln1_g, ln1_b, w_rg, b_rg, w_re, b_re, w1, w3, w2, ln2_g, ln2_b):
    B, n_tok = x.shape[:2]
    cos_a, sin_a = _axial_rope(n_tok, HEAD_DIM)
    cos_c, sin_c = _axial_rope(n_tok, C_ROPE)
    xc = ctx
    for l in range(DEPTH):
        with_ctx = l < DEPTH - 1
        mod = (jax.nn.silu(c) @ w_mod[l] + b_mod[l]).reshape(B, N_MOD, 1, D_MODEL)
        mod_c = (jax.nn.silu(c_ctx) @ w_mod[l] + b_mod[l]).reshape(N_MOD, D_MODEL)
        sh1, sc1, g1, sh2, sc2, g2 = (mod[:, i] for i in range(N_MOD))
        sh1c, sc1c, g1c, sh2c, sc2c, g2c = (mod_c[i] for i in range(N_MOD))

        h = _ln(x) * (1 + sc1) + sh1
        hc = _ln(xc) * (1 + sc1c) + sh1c
        pa, pb, pc = jnp.split(h @ w_in[l], [P_A, P_A + P_B], -1)
        pa_c, pb_c, pc_c = jnp.split(hc @ w_in[l], [P_A, P_A + P_B], -1)
        y_gqa, y_gqa_c = _mixer_gqa(pa, pa_c, q_gain_a[l], k_gain_a[l], cos_a, sin_a, with_ctx)
        y_na, y_na_c = _mixer_na(pb, pb_c, rpb_b[l], with_ctx)
        y_mla, y_mla_c = _mixer_mla(pc, pc_c, q_lat_gain[l], kv_lat_gain[l], w_uq[l], w_ukv[l],
                                    cos_c, sin_c, with_ctx)
        y = jnp.concatenate([y_gqa, y_na, y_mla], -1) @ w_out[l]
        x = _ln_affine(ALPHA * x + g1 * y, ln1_g[l], ln1_b[l])

        h = _ln(x) * (1 + sc2) + sh2
        x = _ln_affine(ALPHA * x + g2 * _hier_moe(h, w_rg[l], b_rg[l], w_re[l], b_re[l], w1[l], w3[l], w2[l]),
                       ln2_g[l], ln2_b[l])

        if with_ctx:
            y_c = jnp.concatenate([y_gqa_c, y_na_c, y_mla_c], -1) @ w_out[l]
            xc = _ln_affine(ALPHA * xc + g1c * y_c, ln1_g[l], ln1_b[l])
            hc = _ln(xc) * (1 + sc2c) + sh2c
            xc = _ln_affine(ALPHA * xc + g2c * _hier_moe(hc, w_rg[l], b_rg[l], w_re[l], b_re[l], w1[l], w3[l], w2[l]),
                            ln2_g[l], ln2_b[l])
    return x
```

```python
import functools

import numpy as np
import jax
import jax.numpy as jnp
from jax import lax
from jax.experimental import pallas as pl
from jax.experimental.pallas import tpu as pltpu

D = 1024
GRID_W = 64
HD = 64
A_HEADS, A_KV = 6, 2
B_HEADS = 5
NA_R, NA_C = 8, 16
C_HEADS, C_QR, C_KVR, C_NOPE, C_ROPE, C_V = 5, 256, 128, 64, 32, 64
THETA = 10000.0
N_GROUPS, EPG, N_EXP, D_EXP = 4, 8, 32, 256
N_PAIR = EPG * (EPG - 1) // 2
N_CLS = N_GROUPS * N_PAIR
DEPTH = 2
ALPHA = (2 * DEPTH) ** 0.25
EPS = 1e-6
N_MOD = 6

LANE = 128
HB = 2 * HD
YW = 3 * HB

TM_IN = 512
TM_OUT = 512
TQ = 256
TM_MOE = 256
T_ROW = 512
VMEM_LIMIT = 56 * 1024 * 1024

F32 = jnp.float32
BF16 = jnp.bfloat16
NEG_INF = float("-inf")


def _dot(a, b):
    return jnp.dot(a, b, preferred_element_type=F32)


def _dot_nt(a, b):
    return lax.dot_general(a, b, (((1,), (1,)), ((), ())), preferred_element_type=F32)


def _params(sem, **kw):
    return pltpu.CompilerParams(dimension_semantics=sem, vmem_limit_bytes=VMEM_LIMIT, **kw)


def _ln_rows(x):
    mu = jnp.mean(x, -1, keepdims=True)
    xc = x - mu
    var = jnp.mean(xc * xc, -1, keepdims=True)
    return xc * lax.rsqrt(var + EPS)


def _silu(a):
    return a * (1.0 / (1.0 + jnp.exp(-a)))


def _mod_kernel(c_ref, w_ref, b_ref, o_ref):
    s = _silu(c_ref[...])
    o_ref[...] = jnp.dot(s, w_ref[...], preferred_element_type=F32, precision=lax.Precision.HIGHEST) + b_ref[...]


def _modulation(cc, w_mod_l, b_mod_l):
    rows = cc.shape[0]
    tn = 1536
    return pl.pallas_call(
        _mod_kernel,
        out_shape=jax.ShapeDtypeStruct((rows, N_MOD * D), F32),
        grid=(N_MOD * D // tn,),
        in_specs=[
            pl.BlockSpec((rows, D), lambda j: (0, 0)),
            pl.BlockSpec((D, tn), lambda j: (0, j)),
            pl.BlockSpec((1, tn), lambda j: (0, j)),
        ],
        out_specs=pl.BlockSpec((rows, tn), lambda j: (0, j)),
        compiler_params=_params(("arbitrary",)),
        name="modulation",
    )(cc, w_mod_l, b_mod_l.reshape(1, -1))


IN_WIDTHS = (YW, 2 * HB, YW, YW, YW, C_QR, 2 * C_KVR)


def _inproj_kernel(x_ref, mod_ref, w_ref, *out_refs):
    ln = _ln_rows(x_ref[...])
    h = (ln * (1.0 + mod_ref[1:2, :]) + mod_ref[0:1, :]).astype(BF16)
    off = 0
    for o_ref, wd in zip(out_refs, IN_WIDTHS):
        o_ref[...] = _dot(h, w_ref[:, off:off + wd]).astype(BF16)
        off += wd


def _mod_index(tile_rows, seq, n_lat, n_batch):
    tiles_per_batch = seq // tile_rows
    n_lat_tiles = n_lat // tile_rows
    return lambda i: (jnp.where(i < n_lat_tiles, i // tiles_per_batch, n_batch), 0, 0)


def _inproj(xp, mod, w_all, seq, n_lat, n_batch):
    n = xp.shape[0]
    return pl.pallas_call(
        _inproj_kernel,
        out_shape=[jax.ShapeDtypeStruct((n, wd), BF16) for wd in IN_WIDTHS],
        grid=(n // TM_IN,),
        in_specs=[
            pl.BlockSpec((TM_IN, D), lambda i: (i, 0)),
            pl.BlockSpec((None, N_MOD, D), _mod_index(TM_IN, seq, n_lat, n_batch)),
            pl.BlockSpec((D, sum(IN_WIDTHS)), lambda i: (0, 0)),
        ],
        out_specs=[pl.BlockSpec((TM_IN, wd), lambda i: (i, 0)) for wd in IN_WIDTHS],
        compiler_params=_params(("arbitrary",)),
        name="inproj",
    )(xp, mod, w_all)


def _softmax_attend(qm, keys, vals, biases=None):
    ss = [_dot_nt(qm, k) for k in keys]
    if biases is not None:
        ss = [s if b is None else s + b for s, b in zip(ss, biases)]
    m = functools.reduce(jnp.maximum, [jnp.max(s, -1, keepdims=True) for s in ss])
    ps = [jnp.exp(s - m) for s in ss]
    den = functools.reduce(jnp.add, [jnp.sum(p, -1, keepdims=True) for p in ps])
    o = functools.reduce(jnp.add, [_dot(p.astype(BF16), v) for p, v in zip(ps, vals)])
    return o * (1.0 / den)


def _first_half(shape):
    return lax.broadcasted_iota(jnp.int32, shape, 1) < HD


def _pair_attend(q, keys, vals, biases=(None, None), n_heads=2):
    first = _first_half(q.shape)
    outs = []
    for half in range(n_heads):
        qm = jnp.where(first if half == 0 else ~first, q, 0.0).astype(BF16)
        outs.append(_softmax_attend(qm, keys, vals, None if biases[half] is None else (biases[half], None)))
    return outs[0] if n_heads == 1 else jnp.where(first, outs[0], outs[1])


def _rope_rows(x, cos, sin, rot_ref):
    return x * cos + _dot(x.astype(BF16), rot_ref[...]) * sin


def _attn_specs(n_batch, seq, ctx_len, q_width, kv_width):
    nq = seq // TQ
    ctx0 = n_batch * seq // ctx_len

    def q_idx(b, i):
        return (jnp.where(i < nq, b * nq + i, ctx0 + b), 0)

    return dict(
        q=lambda w: pl.BlockSpec((TQ, w), q_idx),
        lat=lambda w: pl.BlockSpec((seq, w), lambda b, i: (b, 0)),
        ctx=lambda w: pl.BlockSpec((ctx_len, w), lambda b, i: (ctx0 + b, 0)),
        tab=pl.BlockSpec((TQ, HB), lambda b, i: (i, 0)),
        full=lambda shape: pl.BlockSpec(shape, lambda b, i: (0,) * len(shape)),
    )


def _gqa_kernel(q_ref, kvl_ref, kvc_ref, cosq_ref, sinq_ref, cosk_ref, sink_ref, qg_ref, kg_ref,
                ones_ref, rot_ref, o_ref, kl_s, kc_s, *, nq):
    i = pl.program_id(1)

    def head_rms(x):
        ss = _dot((x * x).astype(BF16), ones_ref[...])
        return lax.rsqrt(ss * (1.0 / HD) + EPS)

    @pl.when(i == 0)
    def _():
        kl = kvl_ref[:, 0:HB].astype(F32)
        kn = kl * head_rms(kl) * kg_ref[...]
        kl_s[...] = _rope_rows(kn, cosk_ref[...], sink_ref[...], rot_ref).astype(BF16)
        kc = kvc_ref[:, 0:HB].astype(F32)
        kc_s[...] = (kc * head_rms(kc) * kg_ref[...]).astype(BF16)

    def run(keys, vals):
        for blk in range(A_HEADS // 2):
            q = q_ref[:, blk * HB:(blk + 1) * HB].astype(F32)
            qn = q * head_rms(q) * qg_ref[...]
            qr = _rope_rows(qn, cosq_ref[...], sinq_ref[...], rot_ref) * HD ** -0.5
            o_ref[:, blk * HB:(blk + 1) * HB] = _pair_attend(qr, keys, vals).astype(BF16)

    @pl.when(i < nq)
    def _():
        run((kl_s[...], kc_s[...]), (kvl_ref[:, HB:2 * HB], kvc_ref[:, HB:2 * HB]))

    @pl.when(i == nq)
    def _():
        run((kc_s[...],), (kvc_ref[:, HB:2 * HB],))


def _gqa(qa, kva, tabs, q_gain, k_gain, consts, seq, ctx_len, n_batch, with_ctx):
    nq = seq // TQ
    sp = _attn_specs(n_batch, seq, ctx_len, YW, 2 * HB)
    n_out = n_batch * (seq + ctx_len) if with_ctx else n_batch * seq
    return pl.pallas_call(
        functools.partial(_gqa_kernel, nq=nq),
        out_shape=jax.ShapeDtypeStruct((n_out, YW), BF16),
        grid=(n_batch, nq + int(with_ctx)),
        in_specs=[
            sp["q"](YW), sp["lat"](2 * HB), sp["ctx"](2 * HB), sp["tab"], sp["tab"],
            sp["full"]((seq, HB)), sp["full"]((seq, HB)),
            sp["full"]((1, HB)), sp["full"]((1, HB)), sp["full"]((HB, HB)), sp["full"]((HB, HB)),
        ],
        out_specs=sp["q"](YW),
        scratch_shapes=[pltpu.VMEM((seq, HB), BF16), pltpu.VMEM((ctx_len, HB), BF16)],
        compiler_params=_params(("arbitrary", "arbitrary")),
        name="gqa",
    )(qa, kva, kva, tabs["cos_a"], tabs["sin_a"], tabs["cos_a"], tabs["sin_a"], q_gain, k_gain,
      consts["ones_a"], consts["rot_a"])


NA_ROWS_PER_TILE = TQ // GRID_W


def _na_kernel(q_ref, kl_ref, vl_ref, kc_ref, vc_ref, bias_ref, o_ref, *, rows, nq):
    i = pl.program_id(1)
    scale = HD ** -0.5
    n_loc = NA_R * GRID_W

    @pl.when(i < nq)
    def _():
        for rr in range(NA_ROWS_PER_TILE):
            r = i * NA_ROWS_PER_TILE + rr
            r0 = jnp.clip(r - NA_R // 2, 0, rows - NA_R)
            dlt = r0 - r + (NA_R - 1)
            k0 = pl.multiple_of(r0 * GRID_W, GRID_W)
            qrows = slice(rr * GRID_W, (rr + 1) * GRID_W)
            for blk in range(3):
                lanes = slice(blk * HB, (blk + 1) * HB)
                q = q_ref[qrows, lanes].astype(F32) * scale
                keys = (kl_ref[pl.ds(k0, n_loc), lanes], kc_ref[:, lanes])
                vals = (vl_ref[pl.ds(k0, n_loc), lanes], vc_ref[:, lanes])
                n_heads = min(2, B_HEADS - 2 * blk)
                biases = tuple(bias_ref[dlt, 2 * blk + half] for half in range(n_heads))
                o_ref[qrows, lanes] = _pair_attend(q, keys, vals, biases, n_heads).astype(BF16)

    @pl.when(i == nq)
    def _():
        for blk in range(3):
            lanes = slice(blk * HB, (blk + 1) * HB)
            q = q_ref[:, lanes].astype(F32) * scale
            n_heads = min(2, B_HEADS - 2 * blk)
            o_ref[:, lanes] = _pair_attend(q, (kc_ref[:, lanes],), (vc_ref[:, lanes],), n_heads=n_heads).astype(BF16)


def _na(qb, kb, vb, bias, seq, ctx_len, n_batch, with_ctx):
    nq = seq // TQ
    sp = _attn_specs(n_batch, seq, ctx_len, YW, YW)
    n_out = n_batch * (seq + ctx_len) if with_ctx else n_batch * seq
    return pl.pallas_call(
        functools.partial(_na_kernel, rows=seq // GRID_W, nq=nq),
        out_shape=jax.ShapeDtypeStruct((n_out, YW), BF16),
        grid=(n_batch, nq + int(with_ctx)),
        in_specs=[sp["q"](YW), sp["lat"](YW), sp["lat"](YW), sp["ctx"](YW), sp["ctx"](YW), sp["full"](bias.shape)],
        out_specs=sp["q"](YW),
        compiler_params=_params(("arbitrary", "arbitrary")),
        name="na",
    )(qb, kb, vb, kb, vb, bias)


def _na_bias(rpb):
    col = jnp.arange(GRID_W, dtype=jnp.int32)
    c0 = jnp.clip(col - NA_C // 2, 0, GRID_W - NA_C)
    col_in = (col[None, :] >= c0[:, None]) & (col[None, :] < c0[:, None] + NA_C)
    dc = jnp.clip(col[None, :] - col[:, None] + (NA_C - 1), 0, 2 * NA_C - 2)
    tbl = jnp.where(col_in[None, None], rpb[:, :, dc], NEG_INF)
    dr = jnp.arange(NA_R)[:, None] + jnp.arange(NA_R)[None, :]
    band = tbl[:, dr]
    return band.transpose(1, 0, 3, 2, 4).reshape(NA_R, B_HEADS, GRID_W, NA_R * GRID_W).astype(F32)


C_QW = C_HEADS * HB


def _mla_kernel(cq_ref, kvl_ref, kvc_ref, cosq_ref, sinq_ref, cosk_ref, sink_ref, qg_ref, kg_ref,
                wuq_ref, wuk_ref, wuv_ref, place_ref, rotq_ref, rotk_ref, o_ref, kl_s, kc_s, vl_s, vc_s, *, nq):
    i = pl.program_id(1)

    def latent_rms(x, gain):
        ms = jnp.mean(x * x, -1, keepdims=True)
        return (x * lax.rsqrt(ms + EPS) * gain).astype(BF16)

    @pl.when(i == 0)
    def _():
        cl = latent_rms(kvl_ref[:, 0:C_KVR].astype(F32), kg_ref[...])
        rl = _rope_rows(kvl_ref[:, C_KVR:2 * C_KVR].astype(F32), cosk_ref[...], sink_ref[...], rotk_ref)
        kl_s[...] = (_dot(cl, wuk_ref[...]) + _dot(rl.astype(BF16), place_ref[...])).astype(BF16)
        vl_s[...] = _dot(cl, wuv_ref[...]).astype(BF16)
        cc = latent_rms(kvc_ref[:, 0:C_KVR].astype(F32), kg_ref[...])
        kc_s[...] = (_dot(cc, wuk_ref[...]) + _dot(kvc_ref[:, C_KVR:2 * C_KVR], place_ref[...])).astype(BF16)
        vc_s[...] = _dot(cc, wuv_ref[...]).astype(BF16)

    def run(with_latent):
        cq = latent_rms(cq_ref[...].astype(F32), qg_ref[...])
        outs = []
        for h in range(C_HEADS):
            lanes = slice(h * HB, (h + 1) * HB)
            q = _dot(cq, wuq_ref[:, lanes])
            qr = _rope_rows(q, cosq_ref[...], sinq_ref[...], rotq_ref) * (C_NOPE + C_ROPE) ** -0.5
            slot = h + 1
            vl = slice((slot // 2) * HB, (slot // 2 + 1) * HB)
            keys, vals = (kc_s[:, lanes],), (vc_s[:, vl],)
            if with_latent:
                keys, vals = (kl_s[:, lanes],) + keys, (vl_s[:, vl],) + vals
            outs.append(_softmax_attend(qr.astype(BF16), keys, vals))
        first = _first_half(outs[0].shape)
        o_ref[:, 0:HB] = outs[0].astype(BF16)
        o_ref[:, HB:2 * HB] = jnp.where(first, outs[1], outs[2]).astype(BF16)
        o_ref[:, 2 * HB:3 * HB] = jnp.where(first, outs[3], outs[4]).astype(BF16)

    @pl.when(i < nq)
    def _():
        run(True)

    @pl.when(i == nq)
    def _():
        run(False)


def _mla(cq, ckv, tabs, q_gain, kv_gain, wts, consts, seq, ctx_len, n_batch, with_ctx):
    nq = seq // TQ
    sp = _attn_specs(n_batch, seq, ctx_len, C_QR, 2 * C_KVR)
    full = sp["full"]
    n_out = n_batch * (seq + ctx_len) if with_ctx else n_batch * seq
    return pl.pallas_call(
        functools.partial(_mla_kernel, nq=nq),
        out_shape=jax.ShapeDtypeStruct((n_out, YW), BF16),
        grid=(n_batch, nq + int(with_ctx)),
        in_specs=[
            sp["q"](C_QR), sp["lat"](2 * C_KVR), sp["ctx"](2 * C_KVR), sp["tab"], sp["tab"],
            full((seq, HB)), full((seq, HB)), full((1, C_QR)), full((1, C_KVR)),
            full((C_QR, C_QW)), full((C_KVR, C_QW)), full((C_KVR, YW)),
            full((HB, C_QW)), full((HB, HB)), full((HB, HB)),
        ],
        out_specs=sp["q"](YW),
        scratch_shapes=[pltpu.VMEM((seq, C_QW), BF16), pltpu.VMEM((ctx_len, C_QW), BF16),
                        pltpu.VMEM((seq, YW), BF16), pltpu.VMEM((ctx_len, YW), BF16)],
        compiler_params=_params(("arbitrary", "arbitrary")),
        name="mla",
    )(cq, ckv, ckv, tabs["cos_cq"], tabs["sin_cq"], tabs["cos_ck"], tabs["sin_ck"], q_gain, kv_gain,
      wts["wuq"], wts["wuk"], wts["wuv"], consts["place_c"], consts["rot_cq"], consts["rot_ck"])


def _outproj_kernel(ya_ref, yb_ref, yc_ref, x_ref, mod_ref, wa_ref, wb_ref, wc_ref, g_ref, b_ref,
                    wr_ref, br_ref, tri_ref, x1_ref, h2_ref, route_ref, cnt_ref, carry_s):
    i = pl.program_id(0)

    @pl.when(i == 0)
    def _():
        carry_s[...] = jnp.zeros_like(carry_s)

    y = _dot(ya_ref[...], wa_ref[...]) + _dot(yb_ref[...], wb_ref[...]) + _dot(yc_ref[...], wc_ref[...])
    x1 = _ln_rows(ALPHA * x_ref[...] + mod_ref[2:3, :] * y) * g_ref[...] + b_ref[...]
    x1_ref[...] = x1
    h2 = _ln_rows(x1) * (1.0 + mod_ref[4:5, :]) + mod_ref[3:4, :]
    h2_ref[...] = h2

    logit = _dot(h2.astype(BF16), wr_ref[...]) + br_ref[...]
    lane = lax.broadcasted_iota(jnp.int32, logit.shape, 1)
    big = jnp.int32(1 << 20)

    def row_max(mask):
        return jnp.max(jnp.where(mask, logit, NEG_INF), -1, keepdims=True)

    def first_lane(mask):
        return jnp.min(jnp.where(mask, lane, big), -1, keepdims=True)

    g_mask = lane < N_GROUPS
    g_max = row_max(g_mask)
    g_idx = first_lane(g_mask & (logit == g_max))
    g_w = 1.0 / jnp.sum(jnp.where(g_mask, jnp.exp(logit - g_max), 0.0), -1, keepdims=True)
    e_base = EPG + EPG * g_idx
    e_mask = (lane >= e_base) & (lane < e_base + EPG)
    m1 = row_max(e_mask)
    l1 = first_lane(e_mask & (logit == m1))
    e_mask2 = e_mask & (lane != l1)
    m2 = row_max(e_mask2)
    l2 = first_lane(e_mask2 & (logit == m2))
    t = jnp.exp(m2 - m1)
    w1 = g_w / (1.0 + t)
    w2 = g_w * t / (1.0 + t)
    i1, i2 = l1 - e_base, l2 - e_base
    lo, hi = jnp.minimum(i1, i2), jnp.maximum(i1, i2)
    pair = lax.shift_right_logical(lo * (2 * EPG - 1 - lo), 1) + hi - lo - 1
    cls = g_idx * N_PAIR + pair
    w_lo = jnp.where(i1 < i2, w1, w2)
    w_hi = jnp.where(i1 < i2, w2, w1)

    onehot = (lane == cls).astype(F32)
    prefix = _dot(tri_ref[...], onehot.astype(BF16)) + carry_s[0:1, :]
    rank = jnp.sum(onehot * prefix, -1, keepdims=True)
    carry_s[...] = carry_s[...] + jnp.sum(onehot, 0, keepdims=True)
    cnt_ref[...] = carry_s[...]

    route_ref[...] = jnp.where(lane == 0, cls.astype(F32),
                               jnp.where(lane == 1, rank,
                                         jnp.where(lane == 2, w_lo, jnp.where(lane == 3, w_hi, 0.0))))


def _outproj(ya, yb, yc, xp, mod, w_parts, ln_g, ln_b, w_r, b_r, tri, n_rows, seq, n_lat, n_batch):
    row = lambda w: pl.BlockSpec((TM_OUT, w), lambda i: (i, 0))
    full = lambda shape: pl.BlockSpec(shape, lambda i: (0, 0))
    return pl.pallas_call(
        _outproj_kernel,
        out_shape=[
            jax.ShapeDtypeStruct((n_rows, D), F32),
            jax.ShapeDtypeStruct((n_rows, D), F32),
            jax.ShapeDtypeStruct((n_rows, LANE), F32),
            jax.ShapeDtypeStruct((8, LANE), F32),
        ],
        grid=(n_rows // TM_OUT,),
        in_specs=[
            row(YW), row(YW), row(YW), row(D),
            pl.BlockSpec((None, N_MOD, D), _mod_index(TM_OUT, seq, n_lat, n_batch)),
            full((YW, D)), full((YW, D)), full((YW, D)),
            full((1, D)), full((1, D)), full((D, LANE)), full((1, LANE)), full((TM_OUT, TM_OUT)),
        ],
        out_specs=[row(D), row(D), row(LANE), full((8, LANE))],
        scratch_shapes=[pltpu.VMEM((8, LANE), F32)],
        compiler_params=_params(("arbitrary",)),
        name="outproj",
    )(ya, yb, yc, xp, mod, *w_parts, ln_g, ln_b, w_r, b_r, tri)


def _dispatch_kernel(pos_hbm, h_hbm, hs_in, hs_out, idx_s, sem_idx, sem_rows):
    del hs_in
    i = pl.program_id(0)
    cp = pltpu.make_async_copy(pos_hbm.at[i], idx_s, sem_idx)
    cp.start()
    cp.wait()
    base = i * T_ROW

    def row_copy(t):
        return pltpu.make_async_copy(h_hbm.at[pl.ds(base + t, 1)], hs_out.at[pl.ds(idx_s[0, t], 1)], sem_rows)

    def start(t, c):
        row_copy(t).start()
        return c

    def wait(t, c):
        row_copy(t).wait()
        return c

    lax.fori_loop(0, T_ROW, start, 0)
    lax.fori_loop(0, T_ROW, wait, 0)


def _dispatch(pos, h2, hs0):
    return pl.pallas_call(
        _dispatch_kernel,
        out_shape=jax.ShapeDtypeStruct(hs0.shape, hs0.dtype),
        grid=(pos.shape[0],),
        in_specs=[pl.BlockSpec(memory_space=pl.ANY)] * 3,
        out_specs=pl.BlockSpec(memory_space=pl.ANY),
        scratch_shapes=[pltpu.SMEM((1, T_ROW), jnp.int32), pltpu.SemaphoreType.DMA, pltpu.SemaphoreType.DMA],
        input_output_aliases={2: 0},
        compiler_params=_params(("arbitrary",)),
        name="dispatch",
    )(pos, h2, hs0)


def _moe_kernel(elo_ref, ehi_ref, valid_ref, h_ref, w1a, w3a, w2a, w1b, w3b, w2b, o_ref):
    j = pl.program_id(0)

    def ffn_bits(h, w1, w3, w2):
        hid = (_silu(_dot(h, w1[...])) * _dot(h, w3[...])).astype(BF16)
        y = _dot(hid, w2[...])
        return lax.bitcast_convert_type(y.astype(BF16).astype(F32), jnp.uint32)

    @pl.when(valid_ref[j] != 0)
    def _():
        h = h_ref[...].astype(BF16)
        y_lo = ffn_bits(h, w1a, w3a, w2a)
        y_hi = ffn_bits(h, w1b, w3b, w2b)
        o_ref[...] = lax.shift_right_logical(y_lo, jnp.uint32(16)) | (y_hi & jnp.uint32(0xFFFF0000))

    @pl.when(valid_ref[j] == 0)
    def _():
        o_ref[...] = jnp.zeros_like(o_ref)


def _moe(tile_elo, tile_ehi, tile_valid, hs, w1, w3, w2):
    lo_spec = lambda shape: pl.BlockSpec((None,) + shape, lambda j, elo, ehi, v: (elo[j], 0, 0))
    hi_spec = lambda shape: pl.BlockSpec((None,) + shape, lambda j, elo, ehi, v: (ehi[j], 0, 0))
    up, down = (D, D_EXP), (D_EXP, D)
    return pl.pallas_call(
        _moe_kernel,
        out_shape=jax.ShapeDtypeStruct((hs.shape[0], D), jnp.uint32),
        grid_spec=pltpu.PrefetchScalarGridSpec(
            num_scalar_prefetch=3,
            grid=(hs.shape[0] // TM_MOE,),
            in_specs=[
                pl.BlockSpec((TM_MOE, D), lambda j, elo, ehi, v: (j, 0)),
                lo_spec(up), lo_spec(up), lo_spec(down), hi_spec(up), hi_spec(up), hi_spec(down),
            ],
            out_specs=pl.BlockSpec((TM_MOE, D), lambda j, elo, ehi, v: (j, 0)),
        ),
        compiler_params=_params(("arbitrary",)),
        name="moe_ffn",
    )(tile_elo, tile_ehi, tile_valid, hs, w1, w3, w2, w1, w3, w2)


def _combine_kernel(pos_hbm, ys_hbm, x1_ref, route_ref, mod_ref, g_ref, b_ref, o_ref, idx_s, buf, sem_idx, sem_rows):
    i = pl.program_id(0)
    cp = pltpu.make_async_copy(pos_hbm.at[i], idx_s, sem_idx)
    cp.start()
    cp.wait()

    def row_copy(t):
        return pltpu.make_async_copy(ys_hbm.at[pl.ds(idx_s[0, t], 1)], buf.at[pl.ds(t, 1)], sem_rows)

    def start(t, c):
        row_copy(t).start()
        return c

    def wait(t, c):
        row_copy(t).wait()
        return c

    lax.fori_loop(0, T_ROW, start, 0)
    lax.fori_loop(0, T_ROW, wait, 0)

    packed = buf[...]
    y_lo = lax.bitcast_convert_type(lax.shift_left(packed, jnp.uint32(16)), F32)
    y_hi = lax.bitcast_convert_type(packed & jnp.uint32(0xFFFF0000), F32)
    moe = route_ref[:, 2:3] * y_lo + route_ref[:, 3:4] * y_hi
    o_ref[...] = _ln_rows(ALPHA * x1_ref[...] + mod_ref[5:6, :] * moe) * g_ref[...] + b_ref[...]


def _combine(pos, ys, x1, route, mod, ln_g, ln_b, seq, n_lat, n_batch):
    n_rows = x1.shape[0]
    return pl.pallas_call(
        _combine_kernel,
        out_shape=jax.ShapeDtypeStruct((n_rows, D), F32),
        grid=(n_rows // T_ROW,),
        in_specs=[
            pl.BlockSpec(memory_space=pl.ANY), pl.BlockSpec(memory_space=pl.ANY),
            pl.BlockSpec((T_ROW, D), lambda i: (i, 0)),
            pl.BlockSpec((T_ROW, LANE), lambda i: (i, 0)),
            pl.BlockSpec((None, N_MOD, D), _mod_index(T_ROW, seq, n_lat, n_batch)),
            pl.BlockSpec((1, D), lambda i: (0, 0)), pl.BlockSpec((1, D), lambda i: (0, 0)),
        ],
        out_specs=pl.BlockSpec((T_ROW, D), lambda i: (i, 0)),
        scratch_shapes=[pltpu.SMEM((1, T_ROW), jnp.int32), pltpu.VMEM((T_ROW, D), jnp.uint32),
                        pltpu.SemaphoreType.DMA, pltpu.SemaphoreType.DMA],
        compiler_params=_params(("arbitrary",)),
        name="combine",
    )(pos, ys, x1, route, mod, ln_g, ln_b)


def _rot_matrix(width, start, half):
    r = np.zeros((width, width), np.float32)
    for j in range(half):
        r[start + half + j, start + j] = -1.0
        r[start + j, start + half + j] = 1.0
    return r


def _constants():
    ones_a = np.kron(np.eye(2, dtype=np.float32), np.ones((HD, HD), np.float32))
    rot_a = _rot_matrix(HB, 0, HD // 2) + _rot_matrix(HB, HD, HD // 2)
    rot_cq = _rot_matrix(HB, C_NOPE, C_ROPE // 2)
    rot_ck = _rot_matrix(HB, 0, C_ROPE // 2)
    place = np.zeros((HB, C_QW), np.float32)
    for h in range(C_HEADS):
        for j in range(C_ROPE):
            place[j, h * HB + C_NOPE + j] = 1.0
    tri = np.tril(np.ones((TM_OUT, TM_OUT), np.float32), -1)
    cls_lo, cls_hi = [], []
    for g in range(N_GROUPS):
        for lo in range(EPG):
            for hi in range(lo + 1, EPG):
                cls_lo.append(g * EPG + lo)
                cls_hi.append(g * EPG + hi)
    as_bf = lambda a: jnp.asarray(a, BF16)
    return dict(ones_a=as_bf(ones_a), rot_a=as_bf(rot_a), rot_cq=as_bf(rot_cq), rot_ck=as_bf(rot_ck),
                place_c=as_bf(place), tri=as_bf(tri),
                cls_lo=jnp.asarray(cls_lo, jnp.int32), cls_hi=jnp.asarray(cls_hi, jnp.int32))


def _rope_tables(seq):
    t = jnp.arange(seq, dtype=jnp.int32)
    row = (t // GRID_W).astype(F32)
    col = (t % GRID_W).astype(F32)

    def angles(dim):
        n_freq = dim // 4
        inv = THETA ** (-jnp.arange(n_freq, dtype=F32) / n_freq)
        ang = jnp.concatenate([row[:, None] * inv, col[:, None] * inv], -1)
        return jnp.concatenate([ang, jnp.zeros((TQ, dim // 2), F32)], 0)

    ang_a, ang_c = angles(HD), angles(C_ROPE)
    ones = lambda w: jnp.ones((seq + TQ, w), F32)
    zeros = lambda w: jnp.zeros((seq + TQ, w), F32)
    ca, sa = jnp.cos(ang_a), jnp.sin(ang_a)
    cc, sc = jnp.cos(ang_c), jnp.sin(ang_c)
    return dict(
        cos_a=jnp.tile(ca, (1, 4)), sin_a=jnp.tile(sa, (1, 4)),
        cos_cq=jnp.concatenate([ones(C_NOPE), cc, cc, ones(HB - C_NOPE - C_ROPE)], -1),
        sin_cq=jnp.concatenate([zeros(C_NOPE), sc, sc, zeros(HB - C_NOPE - C_ROPE)], -1),
        cos_ck=jnp.concatenate([cc, cc, ones(HB - C_ROPE)], -1),
        sin_ck=jnp.concatenate([sc, sc, zeros(HB - C_ROPE)], -1),
    )


def _pad_cols(w, width):
    return jnp.pad(w, ((0, 0), (0, width - w.shape[1])))


def _layer_weights(w_in, w_uq, w_ukv, w_out, w_rg, b_rg, w_re, b_re):
    pa = (A_HEADS + 2 * A_KV) * HD
    pb = 3 * B_HEADS * HD
    bw = B_HEADS * HD
    g = A_HEADS // A_KV
    qa = w_in[:, :A_HEADS * HD].reshape(D, A_KV, g, HD).transpose(0, 2, 1, 3).reshape(D, A_HEADS * HD)
    kva = w_in[:, A_HEADS * HD:pa]
    qb, kb, vb = (_pad_cols(w_in[:, pa + k * bw:pa + (k + 1) * bw], YW) for k in range(3))
    cq = w_in[:, pa + pb:pa + pb + C_QR]
    ckv = _pad_cols(w_in[:, pa + pb + C_QR:], 2 * C_KVR)
    w_all = jnp.concatenate([qa, kva, qb, kb, vb, cq, ckv], -1).astype(BF16)

    wuq = jnp.pad(w_uq.reshape(C_QR, C_HEADS, C_NOPE + C_ROPE), ((0, 0), (0, 0), (0, HB - C_NOPE - C_ROPE)))
    wuq = wuq.reshape(C_QR, C_QW).astype(BF16)
    ukv = w_ukv.reshape(C_KVR, C_HEADS, C_NOPE + C_V)
    wuk = jnp.pad(ukv[:, :, :C_NOPE], ((0, 0), (0, 0), (0, HB - C_NOPE))).reshape(C_KVR, C_QW).astype(BF16)
    wuv = jnp.pad(ukv[:, :, C_NOPE:].reshape(C_KVR, C_HEADS * C_V), ((0, 0), (C_V, 0))).astype(BF16)

    oa = w_out[:A_HEADS * HD].reshape(A_KV, g, HD, D).transpose(1, 0, 2, 3).reshape(A_HEADS * HD, D)
    ob = jnp.pad(w_out[A_HEADS * HD:A_HEADS * HD + bw], ((0, YW - bw), (0, 0)))
    oc = jnp.pad(w_out[A_HEADS * HD + bw:], ((C_V, 0), (0, 0)))
    w_parts = tuple(w.astype(BF16) for w in (oa, ob, oc))

    w_r = jnp.zeros((D, LANE), F32).at[:, :N_GROUPS].set(w_rg).at[:, EPG:EPG + N_EXP].set(w_re).astype(BF16)
    b_r = jnp.zeros((1, LANE), F32).at[0, :N_GROUPS].set(b_rg).at[0, EPG:EPG + N_EXP].set(b_re)
    return w_all, dict(wuq=wuq, wuk=wuk, wuv=wuv), w_parts, w_r, b_r


def kernel(x, c, ctx, c_ctx, w_mod, b_mod, w_in, q_gain_a, k_gain_a, rpb_b, q_lat_gain, kv_lat_gain,
           w_uq, w_ukv, w_out, ln1_g, ln1_b, w_rg, b_rg, w_re, b_re, w1, w3, w2, ln2_g, ln2_b):
    n_batch, seq, _ = x.shape
    ctx_len = ctx.shape[1]
    n_lat, n_ctx = n_batch * seq, n_batch * ctx_len
    assert ctx_len == TQ and seq % TM_IN == 0 and n_ctx % T_ROW == 0 and seq // GRID_W >= NA_R

    consts = _constants()
    tabs = _rope_tables(seq)
    mod_rows = -(-(n_batch + 1) // 8) * 8
    cc = jnp.zeros((mod_rows, D), F32).at[:n_batch].set(c).at[n_batch].set(c_ctx)
    xp = jnp.concatenate([x.reshape(n_lat, D), ctx.reshape(n_ctx, D)], 0)

    for l in range(DEPTH):
        with_ctx = l < DEPTH - 1
        mod = _modulation(cc, w_mod[l], b_mod[l]).reshape(mod_rows, N_MOD, D)
        w_all, mla_w, w_parts, w_r, b_r = _layer_weights(w_in[l], w_uq[l], w_ukv[l], w_out[l],
                                                        w_rg[l], b_rg[l], w_re[l], b_re[l])
        qa, kva, qb, kb, vb, cq, ckv = _inproj(xp, mod, w_all, seq, n_lat, n_batch)
        ya = _gqa(qa, kva, tabs, jnp.tile(q_gain_a[l], 2)[None], jnp.tile(k_gain_a[l], 2)[None],
                  consts, seq, ctx_len, n_batch, with_ctx)
        yb = _na(qb, kb, vb, _na_bias(rpb_b[l]), seq, ctx_len, n_batch, with_ctx)
        yc = _mla(cq, ckv, tabs, q_lat_gain[l][None], kv_lat_gain[l][None], mla_w, consts,
                  seq, ctx_len, n_batch, with_ctx)

        n_rows = n_lat + n_ctx if with_ctx else n_lat
        x1, h2, route, counts = _outproj(ya, yb, yc, xp, mod, w_parts, ln1_g[l][None], ln1_b[l][None],
                                         w_r, b_r, consts["tri"], n_rows, seq, n_lat, n_batch)

        cnt = counts[0, :N_CLS].astype(jnp.int32)
        padded = (cnt + TM_MOE - 1) // TM_MOE * TM_MOE
        ends = jnp.cumsum(padded)
        starts = ends - padded
        n_tiles = n_rows // TM_MOE + N_CLS
        tile_start = jnp.arange(n_tiles, dtype=jnp.int32) * TM_MOE
        tile_cls = jnp.minimum(jnp.searchsorted(ends, tile_start, side="right"), N_CLS - 1).astype(jnp.int32)
        tile_valid = (tile_start < ends[-1]).astype(jnp.int32)
        cls = route[:, 0].astype(jnp.int32)
        pos = (starts[cls] + route[:, 1].astype(jnp.int32)).reshape(n_rows // T_ROW, 1, T_ROW)

        hs = _dispatch(pos, h2, jnp.zeros((n_tiles * TM_MOE, D), F32))
        ys = _moe(consts["cls_lo"][tile_cls], consts["cls_hi"][tile_cls], tile_valid, hs,
                  w1[l].astype(BF16), w3[l].astype(BF16), w2[l].astype(BF16))
        xp = _combine(pos, ys, x1, route, mod, ln2_g[l][None], ln2_b[l][None], seq, n_lat, n_batch)

    return xp.reshape(n_batch, seq, D)
```

```python
import functools

import numpy as np
import jax
import jax.numpy as jnp
from jax import lax
from jax.experimental import pallas as pl
from jax.experimental.pallas import tpu as pltpu

D = 1024
GRID_W = 64
HD = 64
A_HEADS, A_KV = 6, 2
B_HEADS = 5
NA_R, NA_C = 8, 16
C_HEADS, C_QR, C_KVR, C_NOPE, C_ROPE, C_V = 5, 256, 128, 64, 32, 64
THETA = 10000.0
N_GROUPS, EPG, N_EXP, D_EXP = 4, 8, 32, 256
N_PAIR = EPG * (EPG - 1) // 2
N_CLS = N_GROUPS * N_PAIR
DEPTH = 2
ALPHA = (2 * DEPTH) ** 0.25
EPS = 1e-6
N_MOD = 6

LANE = 128
HB = 2 * HD
YW = 3 * HB

TM_IN = 512
TM_OUT = 512
TQ = 256
TM_MOE = 256
T_ROW = 512
VMEM_LIMIT = 56 * 1024 * 1024

F32 = jnp.float32
BF16 = jnp.bfloat16
NEG_INF = float("-inf")
LOG2E = 1.4426950408889634


def _dot(a, b):
    return jnp.dot(a, b, preferred_element_type=F32)


def _dot_nt(a, b):
    return lax.dot_general(a, b, (((1,), (1,)), ((), ())), preferred_element_type=F32)


def _params(sem, **kw):
    return pltpu.CompilerParams(dimension_semantics=sem, vmem_limit_bytes=VMEM_LIMIT, **kw)


def _ln_rows(x):
    mu = jnp.mean(x, -1, keepdims=True)
    xc = x - mu
    var = jnp.mean(xc * xc, -1, keepdims=True)
    return xc * lax.rsqrt(var + EPS)


def _silu(a):
    return a * (1.0 / (1.0 + jnp.exp(-a)))


def _mod_kernel(c_ref, w_ref, b_ref, o_ref):
    s = _silu(c_ref[...])
    o_ref[...] = jnp.dot(s, w_ref[...], preferred_element_type=F32, precision=lax.Precision.HIGHEST) + b_ref[...]


def _modulation(cc, w_mod_l, b_mod_l):
    rows = cc.shape[0]
    tn = 1536
    return pl.pallas_call(
        _mod_kernel,
        out_shape=jax.ShapeDtypeStruct((rows, N_MOD * D), F32),
        grid=(N_MOD * D // tn,),
        in_specs=[
            pl.BlockSpec((rows, D), lambda j: (0, 0)),
            pl.BlockSpec((D, tn), lambda j: (0, j)),
            pl.BlockSpec((1, tn), lambda j: (0, j)),
        ],
        out_specs=pl.BlockSpec((rows, tn), lambda j: (0, j)),
        compiler_params=_params(("arbitrary",)),
        name="modulation",
    )(cc, w_mod_l, b_mod_l.reshape(1, -1))


IN_WIDTHS = (YW, 2 * HB, YW, YW, YW, C_QR, 2 * C_KVR)


def _inproj_kernel(x_ref, mod_ref, w_ref, *out_refs):
    ln = _ln_rows(x_ref[...])
    h = (ln * (1.0 + mod_ref[1:2, :]) + mod_ref[0:1, :]).astype(BF16)
    off = 0
    for o_ref, wd in zip(out_refs, IN_WIDTHS):
        o_ref[...] = _dot(h, w_ref[:, off:off + wd]).astype(BF16)
        off += wd


def _mod_index(tile_rows, seq, n_lat, n_batch):
    tiles_per_batch = seq // tile_rows
    n_lat_tiles = n_lat // tile_rows
    return lambda i: (jnp.where(i < n_lat_tiles, i // tiles_per_batch, n_batch), 0, 0)


def _inproj(xp, mod, w_all, seq, n_lat, n_batch):
    n = xp.shape[0]
    return pl.pallas_call(
        _inproj_kernel,
        out_shape=[jax.ShapeDtypeStruct((n, wd), BF16) for wd in IN_WIDTHS],
        grid=(n // TM_IN,),
        in_specs=[
            pl.BlockSpec((TM_IN, D), lambda i: (i, 0)),
            pl.BlockSpec((None, N_MOD, D), _mod_index(TM_IN, seq, n_lat, n_batch)),
            pl.BlockSpec((D, sum(IN_WIDTHS)), lambda i: (0, 0)),
        ],
        out_specs=[pl.BlockSpec((TM_IN, wd), lambda i: (i, 0)) for wd in IN_WIDTHS],
        compiler_params=_params(("arbitrary",)),
        name="inproj",
    )(xp, mod, w_all)


ROW_HALF = 128


def _softmax_attend(qm, keys, vals, biases=None):
    ss = [_dot_nt(qm, k) for k in keys]
    if biases is not None:
        ss = [s if b is None else s + b for s, b in zip(ss, biases)]
    n_rows = qm.shape[0]
    dens, p_rows = [], []
    for r0 in range(0, n_rows, ROW_HALF):
        rows = slice(r0, min(r0 + ROW_HALF, n_rows))
        blocks = [s[rows, c:c + LANE] for s in ss for c in range(0, s.shape[1], LANE)]
        m = jnp.max(functools.reduce(jnp.maximum, blocks), -1, keepdims=True)
        m_b = jnp.broadcast_to(m, blocks[0].shape)
        acc, p_blocks = None, []
        for blk in blocks:
            p = jnp.exp2(blk - m_b)
            acc = p if acc is None else acc + p
            p_blocks.append(p.astype(BF16))
        dens.append(jnp.sum(acc, -1, keepdims=True))
        p_rows.append(p_blocks)
    den = jnp.concatenate(dens, 0)
    o, c0 = None, 0
    for s, v in zip(ss, vals):
        nb = s.shape[1] // LANE
        p = jnp.concatenate([jnp.concatenate(pr[c0:c0 + nb], 1) for pr in p_rows], 0)
        o = _dot(p, v) if o is None else o + _dot(p, v)
        c0 += nb
    return o * (1.0 / den)


def _first_half(shape):
    return lax.broadcasted_iota(jnp.int32, shape, 1) < HD


def _pair_attend(q, keys, vals, biases=(None, None), n_heads=2):
    first = _first_half(q.shape)
    outs = []
    for half in range(n_heads):
        qm = jnp.where(first if half == 0 else ~first, q, 0.0).astype(BF16)
        outs.append(_softmax_attend(qm, keys, vals, None if biases[half] is None else (biases[half], None)))
    return outs[0] if n_heads == 1 else jnp.where(first, outs[0], outs[1])


def _rope_rows(x, cos, sin, rot_ref):
    return x * cos + _dot(x.astype(BF16), rot_ref[...]) * sin


def _attn_specs(n_batch, seq, ctx_len, q_width, kv_width):
    nq = seq // TQ
    ctx0 = n_batch * seq // ctx_len

    def q_idx(b, i):
        return (jnp.where(i < nq, b * nq + i, ctx0 + b), 0)

    return dict(
        q=lambda w: pl.BlockSpec((TQ, w), q_idx),
        lat=lambda w: pl.BlockSpec((seq, w), lambda b, i: (b, 0)),
        ctx=lambda w: pl.BlockSpec((ctx_len, w), lambda b, i: (ctx0 + b, 0)),
        tab=pl.BlockSpec((TQ, HB), lambda b, i: (i, 0)),
        full=lambda shape: pl.BlockSpec(shape, lambda b, i: (0,) * len(shape)),
    )


def _gqa_kernel(q_ref, kvl_ref, kvc_ref, cosq_ref, sinq_ref, cosk_ref, sink_ref, qg_ref, kg_ref,
                ones_ref, rot_ref, o_ref, kl_s, kc_s, *, nq):
    i = pl.program_id(1)

    def head_rms(x):
        ss = _dot((x * x).astype(BF16), ones_ref[...])
        return lax.rsqrt(ss * (1.0 / HD) + EPS)

    @pl.when(i == 0)
    def _():
        kl = kvl_ref[:, 0:HB].astype(F32)
        kn = kl * head_rms(kl) * kg_ref[...]
        kl_s[...] = _rope_rows(kn, cosk_ref[...], sink_ref[...], rot_ref).astype(BF16)
        kc = kvc_ref[:, 0:HB].astype(F32)
        kc_s[...] = (kc * head_rms(kc) * kg_ref[...]).astype(BF16)

    def run(keys, vals):
        for blk in range(A_HEADS // 2):
            q = q_ref[:, blk * HB:(blk + 1) * HB].astype(F32)
            qn = q * head_rms(q) * qg_ref[...]
            qr = _rope_rows(qn, cosq_ref[...], sinq_ref[...], rot_ref) * (HD ** -0.5 * LOG2E)
            o_ref[:, blk * HB:(blk + 1) * HB] = _pair_attend(qr, keys, vals).astype(BF16)

    @pl.when(i < nq)
    def _():
        run((kl_s[...], kc_s[...]), (kvl_ref[:, HB:2 * HB], kvc_ref[:, HB:2 * HB]))

    @pl.when(i == nq)
    def _():
        run((kc_s[...],), (kvc_ref[:, HB:2 * HB],))


def _gqa(qa, kva, tabs, q_gain, k_gain, consts, seq, ctx_len, n_batch, with_ctx):
    nq = seq // TQ
    sp = _attn_specs(n_batch, seq, ctx_len, YW, 2 * HB)
    n_out = n_batch * (seq + ctx_len) if with_ctx else n_batch * seq
    return pl.pallas_call(
        functools.partial(_gqa_kernel, nq=nq),
        out_shape=jax.ShapeDtypeStruct((n_out, YW), BF16),
        grid=(n_batch, nq + int(with_ctx)),
        in_specs=[
            sp["q"](YW), sp["lat"](2 * HB), sp["ctx"](2 * HB), sp["tab"], sp["tab"],
            sp["full"]((seq, HB)), sp["full"]((seq, HB)),
            sp["full"]((1, HB)), sp["full"]((1, HB)), sp["full"]((HB, HB)), sp["full"]((HB, HB)),
        ],
        out_specs=sp["q"](YW),
        scratch_shapes=[pltpu.VMEM((seq, HB), BF16), pltpu.VMEM((ctx_len, HB), BF16)],
        compiler_params=_params(("arbitrary", "arbitrary")),
        name="gqa",
    )(qa, kva, kva, tabs["cos_a"], tabs["sin_a"], tabs["cos_a"], tabs["sin_a"], q_gain, k_gain,
      consts["ones_a"], consts["rot_a"])


NA_TILE_ROWS = TQ // GRID_W
NA_BAND = NA_R + NA_TILE_ROWS
NA_KEYS = NA_BAND * GRID_W


def _na_band_start(i, rows):
    return jnp.clip(i * NA_TILE_ROWS - NA_R // 2, 0, rows - NA_BAND)


def _na_kernel(q_ref, kl_ref, vl_ref, kc_ref, vc_ref, bias_ref, o_ref, *, rows, nq):
    i = pl.program_id(1)
    scale = HD ** -0.5 * LOG2E

    def run(key_fn, val_fn, with_bias):
        for blk in range(3):
            lanes = slice(blk * HB, (blk + 1) * HB)
            q = q_ref[:, lanes].astype(F32) * scale
            n_heads = min(2, B_HEADS - 2 * blk)
            biases = tuple(bias_ref[2 * blk + half] if with_bias else None for half in range(n_heads))
            o_ref[:, lanes] = _pair_attend(q, key_fn(lanes), val_fn(lanes), biases, n_heads).astype(BF16)

    @pl.when(i < nq)
    def _():
        k0 = pl.multiple_of(_na_band_start(i, rows) * GRID_W, TQ)
        run(lambda lanes: (kl_ref[pl.ds(k0, NA_KEYS), lanes], kc_ref[:, lanes]),
            lambda lanes: (vl_ref[pl.ds(k0, NA_KEYS), lanes], vc_ref[:, lanes]), True)

    @pl.when(i == nq)
    def _():
        run(lambda lanes: (kc_ref[:, lanes],), lambda lanes: (vc_ref[:, lanes],), False)


def _na_tile_config(i, nq):
    return jnp.where(i == 0, 0, jnp.where(i >= nq - 1, 2, 1))


def _na(qb, kb, vb, bias, seq, ctx_len, n_batch, with_ctx):
    nq = seq // TQ
    sp = _attn_specs(n_batch, seq, ctx_len, YW, YW)
    n_out = n_batch * (seq + ctx_len) if with_ctx else n_batch * seq
    bias_spec = pl.BlockSpec((None,) + bias.shape[1:], lambda b, i: (_na_tile_config(i, nq), 0, 0, 0))
    return pl.pallas_call(
        functools.partial(_na_kernel, rows=seq // GRID_W, nq=nq),
        out_shape=jax.ShapeDtypeStruct((n_out, YW), BF16),
        grid=(n_batch, nq + int(with_ctx)),
        in_specs=[sp["q"](YW), sp["lat"](YW), sp["lat"](YW), sp["ctx"](YW), sp["ctx"](YW), bias_spec],
        out_specs=sp["q"](YW),
        compiler_params=_params(("arbitrary", "arbitrary")),
        name="na",
    )(qb, kb, vb, kb, vb, bias)


def _na_window_pattern(i, rows):
    r = i * NA_TILE_ROWS + np.arange(NA_TILE_ROWS)
    r0 = np.clip(r - NA_R // 2, 0, rows - NA_R)
    krow = int(np.clip(i * NA_TILE_ROWS - NA_R // 2, 0, rows - NA_BAND)) + np.arange(NA_BAND)
    valid = (krow[None, :] >= r0[:, None]) & (krow[None, :] < r0[:, None] + NA_R)
    dr = np.clip(krow[None, :] - r[:, None] + (NA_R - 1), 0, 2 * NA_R - 2)
    return valid, dr


def _na_bias(rpb, rows, nq):
    col = jnp.arange(GRID_W, dtype=jnp.int32)
    c0 = jnp.clip(col - NA_C // 2, 0, GRID_W - NA_C)
    col_in = (col[None, :] >= c0[:, None]) & (col[None, :] < c0[:, None] + NA_C)
    dc = jnp.clip(col[None, :] - col[:, None] + (NA_C - 1), 0, 2 * NA_C - 2)
    tbl = jnp.where(col_in[None, None], rpb[:, :, dc] * LOG2E, NEG_INF)
    patterns = [_na_window_pattern(i, rows) for i in range(nq)]
    for i in range(2, nq - 1):
        assert all(np.array_equal(a, b) for a, b in zip(patterns[i], patterns[1]))
    out = []
    for valid, dr in (patterns[0], patterns[1], patterns[nq - 1]):
        band = jnp.where(jnp.asarray(valid)[None, :, :, None, None], tbl[:, jnp.asarray(dr)], NEG_INF)
        out.append(band.transpose(0, 1, 3, 2, 4).reshape(B_HEADS, TQ, NA_KEYS))
    return jnp.stack(out).astype(F32)


C_QW = C_HEADS * HB


def _mla_kernel(cq_ref, kvl_ref, kvc_ref, cosq_ref, sinq_ref, cosk_ref, sink_ref, qg_ref, kg_ref,
                wuq_ref, wuk_ref, wuv_ref, place_ref, rotq_ref, rotk_ref, o_ref, kl_s, kc_s, vl_s, vc_s, *, nq):
    i = pl.program_id(1)

    def latent_rms(x, gain):
        ms = jnp.mean(x * x, -1, keepdims=True)
        return (x * lax.rsqrt(ms + EPS) * gain).astype(BF16)

    @pl.when(i == 0)
    def _():
        cl = latent_rms(kvl_ref[:, 0:C_KVR].astype(F32), kg_ref[...])
        rl = _rope_rows(kvl_ref[:, C_KVR:2 * C_KVR].astype(F32), cosk_ref[...], sink_ref[...], rotk_ref)
        kl_s[...] = (_dot(cl, wuk_ref[...]) + _dot(rl.astype(BF16), place_ref[...])).astype(BF16)
        vl_s[...] = _dot(cl, wuv_ref[...]).astype(BF16)
        cc = latent_rms(kvc_ref[:, 0:C_KVR].astype(F32), kg_ref[...])
        kc_s[...] = (_dot(cc, wuk_ref[...]) + _dot(kvc_ref[:, C_KVR:2 * C_KVR], place_ref[...])).astype(BF16)
        vc_s[...] = _dot(cc, wuv_ref[...]).astype(BF16)

    def run(with_latent):
        cq = latent_rms(cq_ref[...].astype(F32), qg_ref[...])
        outs = []
        for h in range(C_HEADS):
            lanes = slice(h * HB, (h + 1) * HB)
            q = _dot(cq, wuq_ref[:, lanes])
            qr = _rope_rows(q, cosq_ref[...], sinq_ref[...], rotq_ref) * ((C_NOPE + C_ROPE) ** -0.5 * LOG2E)
            slot = h + 1
            vl = slice((slot // 2) * HB, (slot // 2 + 1) * HB)
            keys, vals = (kc_s[:, lanes],), (vc_s[:, vl],)
            if with_latent:
                keys, vals = (kl_s[:, lanes],) + keys, (vl_s[:, vl],) + vals
            outs.append(_softmax_attend(qr.astype(BF16), keys, vals))
        first = _first_half(outs[0].shape)
        o_ref[:, 0:HB] = outs[0].astype(BF16)
        o_ref[:, HB:2 * HB] = jnp.where(first, outs[1], outs[2]).astype(BF16)
        o_ref[:, 2 * HB:3 * HB] = jnp.where(first, outs[3], outs[4]).astype(BF16)

    @pl.when(i < nq)
    def _():
        run(True)

    @pl.when(i == nq)
    def _():
        run(False)


def _mla(cq, ckv, tabs, q_gain, kv_gain, wts, consts, seq, ctx_len, n_batch, with_ctx):
    nq = seq // TQ
    sp = _attn_specs(n_batch, seq, ctx_len, C_QR, 2 * C_KVR)
    full = sp["full"]
    n_out = n_batch * (seq + ctx_len) if with_ctx else n_batch * seq
    return pl.pallas_call(
        functools.partial(_mla_kernel, nq=nq),
        out_shape=jax.ShapeDtypeStruct((n_out, YW), BF16),
        grid=(n_batch, nq + int(with_ctx)),
        in_specs=[
            sp["q"](C_QR), sp["lat"](2 * C_KVR), sp["ctx"](2 * C_KVR), sp["tab"], sp["tab"],
            full((seq, HB)), full((seq, HB)), full((1, C_QR)), full((1, C_KVR)),
            full((C_QR, C_QW)), full((C_KVR, C_QW)), full((C_KVR, YW)),
            full((HB, C_QW)), full((HB, HB)), full((HB, HB)),
        ],
        out_specs=sp["q"](YW),
        scratch_shapes=[pltpu.VMEM((seq, C_QW), BF16), pltpu.VMEM((ctx_len, C_QW), BF16),
                        pltpu.VMEM((seq, YW), BF16), pltpu.VMEM((ctx_len, YW), BF16)],
        compiler_params=_params(("arbitrary", "arbitrary")),
        name="mla",
    )(cq, ckv, ckv, tabs["cos_cq"], tabs["sin_cq"], tabs["cos_ck"], tabs["sin_ck"], q_gain, kv_gain,
      wts["wuq"], wts["wuk"], wts["wuv"], consts["place_c"], consts["rot_cq"], consts["rot_ck"])


def _outproj_kernel(ya_ref, yb_ref, yc_ref, x_ref, mod_ref, wa_ref, wb_ref, wc_ref, g_ref, b_ref,
                    wr_ref, br_ref, tri_ref, x1_ref, h2_ref, route_ref, cnt_ref, carry_s):
    i = pl.program_id(0)

    @pl.when(i == 0)
    def _():
        carry_s[...] = jnp.zeros_like(carry_s)

    y = _dot(ya_ref[...], wa_ref[...]) + _dot(yb_ref[...], wb_ref[...]) + _dot(yc_ref[...], wc_ref[...])
    x1 = _ln_rows(ALPHA * x_ref[...] + mod_ref[2:3, :] * y) * g_ref[...] + b_ref[...]
    x1_ref[...] = x1
    h2 = _ln_rows(x1) * (1.0 + mod_ref[4:5, :]) + mod_ref[3:4, :]
    h2_ref[...] = h2

    logit = _dot(h2.astype(BF16), wr_ref[...]) + br_ref[...]
    lane = lax.broadcasted_iota(jnp.int32, logit.shape, 1)
    big = jnp.int32(1 << 20)

    def row_max(mask):
        return jnp.max(jnp.where(mask, logit, NEG_INF), -1, keepdims=True)

    def first_lane(mask):
        return jnp.min(jnp.where(mask, lane, big), -1, keepdims=True)

    g_mask = lane < N_GROUPS
    g_max = row_max(g_mask)
    g_idx = first_lane(g_mask & (logit == g_max))
    g_w = 1.0 / jnp.sum(jnp.where(g_mask, jnp.exp(logit - g_max), 0.0), -1, keepdims=True)
    e_base = EPG + EPG * g_idx
    e_mask = (lane >= e_base) & (lane < e_base + EPG)
    m1 = row_max(e_mask)
    l1 = first_lane(e_mask & (logit == m1))
    e_mask2 = e_mask & (lane != l1)
    m2 = row_max(e_mask2)
    l2 = first_lane(e_mask2 & (logit == m2))
    t = jnp.exp(m2 - m1)
    w1 = g_w / (1.0 + t)
    w2 = g_w * t / (1.0 + t)
    i1, i2 = l1 - e_base, l2 - e_base
    lo, hi = jnp.minimum(i1, i2), jnp.maximum(i1, i2)
    pair = lax.shift_right_logical(lo * (2 * EPG - 1 - lo), 1) + hi - lo - 1
    cls = g_idx * N_PAIR + pair
    w_lo = jnp.where(i1 < i2, w1, w2)
    w_hi = jnp.where(i1 < i2, w2, w1)

    onehot = (lane == cls).astype(F32)
    prefix = _dot(tri_ref[...], onehot.astype(BF16)) + carry_s[0:1, :]
    rank = jnp.sum(onehot * prefix, -1, keepdims=True)
    carry_s[...] = carry_s[...] + jnp.sum(onehot, 0, keepdims=True)
    cnt_ref[...] = carry_s[...]

    route_ref[...] = jnp.where(lane == 0, cls.astype(F32),
                               jnp.where(lane == 1, rank,
                                         jnp.where(lane == 2, w_lo, jnp.where(lane == 3, w_hi, 0.0))))


def _outproj(ya, yb, yc, xp, mod, w_parts, ln_g, ln_b, w_r, b_r, tri, n_rows, seq, n_lat, n_batch):
    row = lambda w: pl.BlockSpec((TM_OUT, w), lambda i: (i, 0))
    full = lambda shape: pl.BlockSpec(shape, lambda i: (0, 0))
    return pl.pallas_call(
        _outproj_kernel,
        out_shape=[
            jax.ShapeDtypeStruct((n_rows, D), F32),
            jax.ShapeDtypeStruct((n_rows, D), F32),
            jax.ShapeDtypeStruct((n_rows, LANE), F32),
            jax.ShapeDtypeStruct((8, LANE), F32),
        ],
        grid=(n_rows // TM_OUT,),
        in_specs=[
            row(YW), row(YW), row(YW), row(D),
            pl.BlockSpec((None, N_MOD, D), _mod_index(TM_OUT, seq, n_lat, n_batch)),
            full((YW, D)), full((YW, D)), full((YW, D)),
            full((1, D)), full((1, D)), full((D, LANE)), full((1, LANE)), full((TM_OUT, TM_OUT)),
        ],
        out_specs=[row(D), row(D), row(LANE), full((8, LANE))],
        scratch_shapes=[pltpu.VMEM((8, LANE), F32)],
        compiler_params=_params(("arbitrary",)),
        name="outproj",
    )(ya, yb, yc, xp, mod, *w_parts, ln_g, ln_b, w_r, b_r, tri)


ROW_UNROLL = 8


def _load_row_indices(i, cls_hbm, rank_hbm, cls_s, rank_s, sem_idx):
    copies = [pltpu.make_async_copy(src.at[i], dst, sem_idx.at[k])
              for k, (src, dst) in enumerate(((cls_hbm, cls_s), (rank_hbm, rank_s)))]
    for cp in copies:
        cp.start()
    for cp in copies:
        cp.wait()


def _row_dma_loop(make_copy):
    def start(u, c):
        for k in range(ROW_UNROLL):
            make_copy(u * ROW_UNROLL + k).start(priority=k % 2)
        return c

    def wait(u, c):
        for k in range(ROW_UNROLL):
            make_copy(0).wait()
        return c

    lax.fori_loop(0, T_ROW // ROW_UNROLL, start, 0)
    lax.fori_loop(0, T_ROW // ROW_UNROLL, wait, 0)


def _dispatch_kernel(starts_ref, cls_hbm, rank_hbm, h_ref, hs_in, hs_out, cls_s, rank_s, sem_idx, sem_rows):
    del hs_in
    _load_row_indices(pl.program_id(0), cls_hbm, rank_hbm, cls_s, rank_s, sem_idx)

    def row_copy(t):
        pos = starts_ref[cls_s[0, t]] + rank_s[0, t]
        return pltpu.make_async_copy(h_ref.at[pl.ds(t, 1)], hs_out.at[pl.ds(pos, 1)], sem_rows)

    _row_dma_loop(row_copy)


def _dispatch(starts, cls_i, rank_i, h2, hs0):
    any_spec = pl.BlockSpec(memory_space=pl.ANY)
    return pl.pallas_call(
        _dispatch_kernel,
        out_shape=jax.ShapeDtypeStruct(hs0.shape, hs0.dtype),
        grid_spec=pltpu.PrefetchScalarGridSpec(
            num_scalar_prefetch=1,
            grid=(cls_i.shape[0],),
            in_specs=[any_spec, any_spec, pl.BlockSpec((T_ROW, D), lambda i, st: (i, 0)), any_spec],
            out_specs=any_spec,
            scratch_shapes=[pltpu.SMEM((1, T_ROW), jnp.int32), pltpu.SMEM((1, T_ROW), jnp.int32),
                            pltpu.SemaphoreType.DMA((2,)), pltpu.SemaphoreType.DMA],
        ),
        input_output_aliases={4: 0},
        compiler_params=_params(("arbitrary",)),
        name="dispatch",
    )(starts, cls_i, rank_i, h2, hs0)


def _moe_kernel(elo_ref, ehi_ref, valid_ref, h_ref, w1a, w3a, w2a, w1b, w3b, w2b, o_ref):
    j = pl.program_id(0)

    def ffn_bits(h, w1, w3, w2):
        hid = (_silu(_dot(h, w1[...])) * _dot(h, w3[...])).astype(BF16)
        y = _dot(hid, w2[...])
        return lax.bitcast_convert_type(y.astype(BF16).astype(F32), jnp.uint32)

    @pl.when(valid_ref[j] != 0)
    def _():
        h = h_ref[...].astype(BF16)
        y_lo = ffn_bits(h, w1a, w3a, w2a)
        y_hi = ffn_bits(h, w1b, w3b, w2b)
        o_ref[...] = lax.shift_right_logical(y_lo, jnp.uint32(16)) | (y_hi & jnp.uint32(0xFFFF0000))

    @pl.when(valid_ref[j] == 0)
    def _():
        o_ref[...] = jnp.zeros_like(o_ref)


def _moe(tile_elo, tile_ehi, tile_valid, hs, w1, w3, w2):
    lo_spec = lambda shape: pl.BlockSpec((None,) + shape, lambda j, elo, ehi, v: (elo[j], 0, 0))
    hi_spec = lambda shape: pl.BlockSpec((None,) + shape, lambda j, elo, ehi, v: (ehi[j], 0, 0))
    up, down = (D, D_EXP), (D_EXP, D)
    return pl.pallas_call(
        _moe_kernel,
        out_shape=jax.ShapeDtypeStruct((hs.shape[0], D), jnp.uint32),
        grid_spec=pltpu.PrefetchScalarGridSpec(
            num_scalar_prefetch=3,
            grid=(hs.shape[0] // TM_MOE,),
            in_specs=[
                pl.BlockSpec((TM_MOE, D), lambda j, elo, ehi, v: (j, 0)),
                lo_spec(up), lo_spec(up), lo_spec(down), hi_spec(up), hi_spec(up), hi_spec(down),
            ],
            out_specs=pl.BlockSpec((TM_MOE, D), lambda j, elo, ehi, v: (j, 0)),
        ),
        compiler_params=_params(("arbitrary",)),
        name="moe_ffn",
    )(tile_elo, tile_ehi, tile_valid, hs, w1, w3, w2, w1, w3, w2)


def _combine_kernel(starts_ref, cls_hbm, rank_hbm, ys_hbm, x1_ref, route_ref, mod_ref, g_ref, b_ref, o_ref,
                    cls_s, rank_s, buf, sem_idx, sem_rows):
    _load_row_indices(pl.program_id(0), cls_hbm, rank_hbm, cls_s, rank_s, sem_idx)

    def row_copy(t):
        pos = starts_ref[cls_s[0, t]] + rank_s[0, t]
        return pltpu.make_async_copy(ys_hbm.at[pl.ds(pos, 1)], buf.at[pl.ds(t, 1)], sem_rows)

    _row_dma_loop(row_copy)

    packed = buf[...]
    y_lo = lax.bitcast_convert_type(lax.shift_left(packed, jnp.uint32(16)), F32)
    y_hi = lax.bitcast_convert_type(packed & jnp.uint32(0xFFFF0000), F32)
    moe = route_ref[:, 2:3] * y_lo + route_ref[:, 3:4] * y_hi
    o_ref[...] = _ln_rows(ALPHA * x1_ref[...] + mod_ref[5:6, :] * moe) * g_ref[...] + b_ref[...]


def _combine(starts, cls_i, rank_i, ys, x1, route, mod, ln_g, ln_b, seq, n_lat, n_batch):
    n_rows = x1.shape[0]
    any_spec = pl.BlockSpec(memory_space=pl.ANY)
    mod_idx = _mod_index(T_ROW, seq, n_lat, n_batch)
    return pl.pallas_call(
        _combine_kernel,
        out_shape=jax.ShapeDtypeStruct((n_rows, D), F32),
        grid_spec=pltpu.PrefetchScalarGridSpec(
            num_scalar_prefetch=1,
            grid=(n_rows // T_ROW,),
            in_specs=[
                any_spec, any_spec, any_spec,
                pl.BlockSpec((T_ROW, D), lambda i, st: (i, 0)),
                pl.BlockSpec((T_ROW, LANE), lambda i, st: (i, 0)),
                pl.BlockSpec((None, N_MOD, D), lambda i, st: mod_idx(i)),
                pl.BlockSpec((1, D), lambda i, st: (0, 0)), pl.BlockSpec((1, D), lambda i, st: (0, 0)),
            ],
            out_specs=pl.BlockSpec((T_ROW, D), lambda i, st: (i, 0)),
            scratch_shapes=[pltpu.SMEM((1, T_ROW), jnp.int32), pltpu.SMEM((1, T_ROW), jnp.int32),
                            pltpu.VMEM((T_ROW, D), jnp.uint32),
                            pltpu.SemaphoreType.DMA((2,)), pltpu.SemaphoreType.DMA],
        ),
        compiler_params=_params(("arbitrary",)),
        name="combine",
    )(starts, cls_i, rank_i, ys, x1, route, mod, ln_g, ln_b)


def _rot_matrix(width, start, half):
    r = np.zeros((width, width), np.float32)
    for j in range(half):
        r[start + half + j, start + j] = -1.0
        r[start + j, start + half + j] = 1.0
    return r


def _constants():
    ones_a = np.kron(np.eye(2, dtype=np.float32), np.ones((HD, HD), np.float32))
    rot_a = _rot_matrix(HB, 0, HD // 2) + _rot_matrix(HB, HD, HD // 2)
    rot_cq = _rot_matrix(HB, C_NOPE, C_ROPE // 2)
    rot_ck = _rot_matrix(HB, 0, C_ROPE // 2)
    place = np.zeros((HB, C_QW), np.float32)
    for h in range(C_HEADS):
        for j in range(C_ROPE):
            place[j, h * HB + C_NOPE + j] = 1.0
    tri = np.tril(np.ones((TM_OUT, TM_OUT), np.float32), -1)
    cls_lo, cls_hi = [], []
    for g in range(N_GROUPS):
        for lo in range(EPG):
            for hi in range(lo + 1, EPG):
                cls_lo.append(g * EPG + lo)
                cls_hi.append(g * EPG + hi)
    as_bf = lambda a: jnp.asarray(a, BF16)
    return dict(ones_a=as_bf(ones_a), rot_a=as_bf(rot_a), rot_cq=as_bf(rot_cq), rot_ck=as_bf(rot_ck),
                place_c=as_bf(place), tri=as_bf(tri),
                cls_lo=jnp.asarray(cls_lo, jnp.int32), cls_hi=jnp.asarray(cls_hi, jnp.int32))


def _rope_tables(seq):
    t = jnp.arange(seq, dtype=jnp.int32)
    row = (t // GRID_W).astype(F32)
    col = (t % GRID_W).astype(F32)

    def angles(dim):
        n_freq = dim // 4
        inv = THETA ** (-jnp.arange(n_freq, dtype=F32) / n_freq)
        ang = jnp.concatenate([row[:, None] * inv, col[:, None] * inv], -1)
        return jnp.concatenate([ang, jnp.zeros((TQ, dim // 2), F32)], 0)

    ang_a, ang_c = angles(HD), angles(C_ROPE)
    ones = lambda w: jnp.ones((seq + TQ, w), F32)
    zeros = lambda w: jnp.zeros((seq + TQ, w), F32)
    ca, sa = jnp.cos(ang_a), jnp.sin(ang_a)
    cc, sc = jnp.cos(ang_c), jnp.sin(ang_c)
    return dict(
        cos_a=jnp.tile(ca, (1, 4)), sin_a=jnp.tile(sa, (1, 4)),
        cos_cq=jnp.concatenate([ones(C_NOPE), cc, cc, ones(HB - C_NOPE - C_ROPE)], -1),
        sin_cq=jnp.concatenate([zeros(C_NOPE), sc, sc, zeros(HB - C_NOPE - C_ROPE)], -1),
        cos_ck=jnp.concatenate([cc, cc, ones(HB - C_ROPE)], -1),
        sin_ck=jnp.concatenate([sc, sc, zeros(HB - C_ROPE)], -1),
    )


def _pad_cols(w, width):
    return jnp.pad(w, ((0, 0), (0, width - w.shape[1])))


def _layer_weights(w_in, w_uq, w_ukv, w_out, w_rg, b_rg, w_re, b_re):
    pa = (A_HEADS + 2 * A_KV) * HD
    pb = 3 * B_HEADS * HD
    bw = B_HEADS * HD
    g = A_HEADS // A_KV
    qa = w_in[:, :A_HEADS * HD].reshape(D, A_KV, g, HD).transpose(0, 2, 1, 3).reshape(D, A_HEADS * HD)
    kva = w_in[:, A_HEADS * HD:pa]
    qb, kb, vb = (_pad_cols(w_in[:, pa + k * bw:pa + (k + 1) * bw], YW) for k in range(3))
    cq = w_in[:, pa + pb:pa + pb + C_QR]
    ckv = _pad_cols(w_in[:, pa + pb + C_QR:], 2 * C_KVR)
    w_all = jnp.concatenate([qa, kva, qb, kb, vb, cq, ckv], -1).astype(BF16)

    wuq = jnp.pad(w_uq.reshape(C_QR, C_HEADS, C_NOPE + C_ROPE), ((0, 0), (0, 0), (0, HB - C_NOPE - C_ROPE)))
    wuq = wuq.reshape(C_QR, C_QW).astype(BF16)
    ukv = w_ukv.reshape(C_KVR, C_HEADS, C_NOPE + C_V)
    wuk = jnp.pad(ukv[:, :, :C_NOPE], ((0, 0), (0, 0), (0, HB - C_NOPE))).reshape(C_KVR, C_QW).astype(BF16)
    wuv = jnp.pad(ukv[:, :, C_NOPE:].reshape(C_KVR, C_HEADS * C_V), ((0, 0), (C_V, 0))).astype(BF16)

    oa = w_out[:A_HEADS * HD].reshape(A_KV, g, HD, D).transpose(1, 0, 2, 3).reshape(A_HEADS * HD, D)
    ob = jnp.pad(w_out[A_HEADS * HD:A_HEADS * HD + bw], ((0, YW - bw), (0, 0)))
    oc = jnp.pad(w_out[A_HEADS * HD + bw:], ((C_V, 0), (0, 0)))
    w_parts = tuple(w.astype(BF16) for w in (oa, ob, oc))

    w_r = jnp.zeros((D, LANE), F32).at[:, :N_GROUPS].set(w_rg).at[:, EPG:EPG + N_EXP].set(w_re).astype(BF16)
    b_r = jnp.zeros((1, LANE), F32).at[0, :N_GROUPS].set(b_rg).at[0, EPG:EPG + N_EXP].set(b_re)
    return w_all, dict(wuq=wuq, wuk=wuk, wuv=wuv), w_parts, w_r, b_r


def kernel(x, c, ctx, c_ctx, w_mod, b_mod, w_in, q_gain_a, k_gain_a, rpb_b, q_lat_gain, kv_lat_gain,
           w_uq, w_ukv, w_out, ln1_g, ln1_b, w_rg, b_rg, w_re, b_re, w1, w3, w2, ln2_g, ln2_b):
    n_batch, seq, _ = x.shape
    ctx_len = ctx.shape[1]
    n_lat, n_ctx = n_batch * seq, n_batch * ctx_len
    assert ctx_len == TQ and seq % TM_IN == 0 and n_ctx % T_ROW == 0 and seq // GRID_W >= NA_BAND

    consts = _constants()
    tabs = _rope_tables(seq)
    mod_rows = -(-(n_batch + 1) // 8) * 8
    cc = jnp.zeros((mod_rows, D), F32).at[:n_batch].set(c).at[n_batch].set(c_ctx)
    xp = jnp.concatenate([x.reshape(n_lat, D), ctx.reshape(n_ctx, D)], 0)

    for l in range(DEPTH):
        with_ctx = l < DEPTH - 1
        mod = _modulation(cc, w_mod[l], b_mod[l]).reshape(mod_rows, N_MOD, D)
        w_all, mla_w, w_parts, w_r, b_r = _layer_weights(w_in[l], w_uq[l], w_ukv[l], w_out[l],
                                                        w_rg[l], b_rg[l], w_re[l], b_re[l])
        qa, kva, qb, kb, vb, cq, ckv = _inproj(xp, mod, w_all, seq, n_lat, n_batch)
        ya = _gqa(qa, kva, tabs, jnp.tile(q_gain_a[l], 2)[None], jnp.tile(k_gain_a[l], 2)[None],
                  consts, seq, ctx_len, n_batch, with_ctx)
        yb = _na(qb, kb, vb, _na_bias(rpb_b[l], seq // GRID_W, seq // TQ), seq, ctx_len, n_batch, with_ctx)
        yc = _mla(cq, ckv, tabs, q_lat_gain[l][None], kv_lat_gain[l][None], mla_w, consts,
                  seq, ctx_len, n_batch, with_ctx)

        n_rows = n_lat + n_ctx if with_ctx else n_lat
        x1, h2, route, counts = _outproj(ya, yb, yc, xp, mod, w_parts, ln1_g[l][None], ln1_b[l][None],
                                         w_r, b_r, consts["tri"], n_rows, seq, n_lat, n_batch)

        cnt = counts[0, :N_CLS].astype(jnp.int32)
        padded = (cnt + TM_MOE - 1) // TM_MOE * TM_MOE
        ends = jnp.cumsum(padded)
        starts = ends - padded
        n_tiles = n_rows // TM_MOE + N_CLS
        tile_start = jnp.arange(n_tiles, dtype=jnp.int32) * TM_MOE
        tile_cls = jnp.sum((ends[None, :] <= tile_start[:, None]).astype(jnp.int32), -1)
        tile_cls = jnp.minimum(tile_cls, N_CLS - 1)
        tile_valid = (tile_start < ends[-1]).astype(jnp.int32)
        cls_i = route[:, 0].astype(jnp.int32).reshape(n_rows // T_ROW, 1, T_ROW)
        rank_i = route[:, 1].astype(jnp.int32).reshape(n_rows // T_ROW, 1, T_ROW)

        hs = _dispatch(starts, cls_i, rank_i, h2, jnp.zeros((n_tiles * TM_MOE, D), F32))
        ys = _moe(consts["cls_lo"][tile_cls], consts["cls_hi"][tile_cls], tile_valid, hs,
                  w1[l].astype(BF16), w3[l].astype(BF16), w2[l].astype(BF16))
        xp = _combine(starts, cls_i, rank_i, ys, x1, route, mod, ln2_g[l][None], ln2_b[l][None],
                      seq, n_lat, n_batch)

    return xp.reshape(n_batch, seq, D)
```

```python
import functools

import numpy as np
import jax
import jax.numpy as jnp
from jax import lax
from jax.experimental import pallas as pl
from jax.experimental.pallas import tpu as pltpu

D = 1024
GRID_W = 64
HD = 64
A_HEADS, A_KV = 6, 2
B_HEADS = 5
NA_R, NA_C = 8, 16
C_HEADS, C_QR, C_KVR, C_NOPE, C_ROPE, C_V = 5, 256, 128, 64, 32, 64
THETA = 10000.0
N_GROUPS, EPG, N_EXP, D_EXP = 4, 8, 32, 256
N_PAIR = EPG * (EPG - 1) // 2
N_CLS = N_GROUPS * N_PAIR
DEPTH = 2
ALPHA = (2 * DEPTH) ** 0.25
EPS = 1e-6
N_MOD = 6

LANE = 128
HB = 2 * HD
YW = 3 * HB

TM_IN = 512
TM_OUT = 512
TQ = 256
TM_MOE = 256
T_ROW = 512
VMEM_LIMIT = 56 * 1024 * 1024

F32 = jnp.float32
BF16 = jnp.bfloat16
NEG_INF = float("-inf")
LOG2E = 1.4426950408889634


def _dot(a, b):
    return jnp.dot(a, b, preferred_element_type=F32)


def _dot_nt(a, b):
    return lax.dot_general(a, b, (((1,), (1,)), ((), ())), preferred_element_type=F32)


def _params(sem, **kw):
    return pltpu.CompilerParams(dimension_semantics=sem, vmem_limit_bytes=VMEM_LIMIT, **kw)


def _ln_rows(x):
    mu = jnp.mean(x, -1, keepdims=True)
    xc = x - mu
    var = jnp.mean(xc * xc, -1, keepdims=True)
    return xc * lax.rsqrt(var + EPS)


def _silu(a):
    return a * (1.0 / (1.0 + jnp.exp(-a)))


def _mod_kernel(c_ref, w_ref, b_ref, o_ref):
    s = _silu(c_ref[...])
    o_ref[...] = jnp.dot(s, w_ref[...], preferred_element_type=F32, precision=lax.Precision.HIGHEST) + b_ref[...]


def _modulation(cc, w_mod_l, b_mod_l):
    rows = cc.shape[0]
    tn = 1536
    return pl.pallas_call(
        _mod_kernel,
        out_shape=jax.ShapeDtypeStruct((rows, N_MOD * D), F32),
        grid=(N_MOD * D // tn,),
        in_specs=[
            pl.BlockSpec((rows, D), lambda j: (0, 0)),
            pl.BlockSpec((D, tn), lambda j: (0, j)),
            pl.BlockSpec((1, tn), lambda j: (0, j)),
        ],
        out_specs=pl.BlockSpec((rows, tn), lambda j: (0, j)),
        compiler_params=_params(("arbitrary",)),
        name="modulation",
    )(cc, w_mod_l, b_mod_l.reshape(1, -1))


IN_WIDTHS = (YW, 2 * HB, YW, YW, YW, C_QR, 2 * C_KVR)


def _inproj_kernel(x_ref, mod_ref, w_ref, *out_refs):
    ln = _ln_rows(x_ref[...])
    h = (ln * (1.0 + mod_ref[1:2, :]) + mod_ref[0:1, :]).astype(BF16)
    p = _dot(h, w_ref[...])
    off = 0
    for o_ref, wd in zip(out_refs, IN_WIDTHS):
        o_ref[...] = p[:, off:off + wd].astype(BF16)
        off += wd


def _mod_index(tile_rows, seq, n_lat, n_batch):
    tiles_per_batch = seq // tile_rows
    n_lat_tiles = n_lat // tile_rows
    return lambda i: (jnp.where(i < n_lat_tiles, i // tiles_per_batch, n_batch), 0, 0)


def _inproj(xp, mod, w_all, seq, n_lat, n_batch):
    n = xp.shape[0]
    return pl.pallas_call(
        _inproj_kernel,
        out_shape=[jax.ShapeDtypeStruct((n, wd), BF16) for wd in IN_WIDTHS],
        grid=(n // TM_IN,),
        in_specs=[
            pl.BlockSpec((TM_IN, D), lambda i: (i, 0)),
            pl.BlockSpec((None, N_MOD, D), _mod_index(TM_IN, seq, n_lat, n_batch)),
            pl.BlockSpec((D, sum(IN_WIDTHS)), lambda i: (0, 0)),
        ],
        out_specs=[pl.BlockSpec((TM_IN, wd), lambda i: (i, 0)) for wd in IN_WIDTHS],
        compiler_params=_params(("arbitrary",)),
        name="inproj",
    )(xp, mod, w_all)


KEY_CHUNK = 64


def _scores(task):
    qm, key_thunks, _, bias_thunks = task
    chunks = [k() for k in key_thunks]
    s = _dot_nt(jnp.concatenate(chunks, 0), qm())
    parts, r0 = [], 0
    for c, k in enumerate(chunks):
        part = s[r0:r0 + k.shape[0]]
        if bias_thunks is not None and bias_thunks[c] is not None:
            part = part + bias_thunks[c]()
        parts.append(part)
        r0 += k.shape[0]
    return parts


def _softmax(parts):
    blocks = [s[c:c + KEY_CHUNK] for s in parts for c in range(0, s.shape[0], KEY_CHUNK)]
    m = jnp.max(functools.reduce(jnp.maximum, blocks), 0, keepdims=True)
    m_b = jnp.broadcast_to(m, blocks[0].shape)
    acc, p_blocks = None, []
    for blk in blocks:
        p = jnp.exp2(blk - m_b)
        acc = p if acc is None else acc + p
        p_blocks.append(p.astype(BF16))
    return jnp.concatenate(p_blocks, 0), jnp.sum(acc, 0, keepdims=True)


def _attend_heads(tasks):
    outs, scored, soft = [], None, None
    for task in list(tasks) + [None, None]:
        nxt = None if task is None else (_scores(task), task[2])
        nxt_soft = None if scored is None else _softmax(scored[0]) + (scored[1],)
        if soft is not None:
            p, den, vt_thunks = soft
            outs.append(_dot(jnp.concatenate([v() for v in vt_thunks], 1), p) * (1.0 / den))
        scored, soft = nxt, nxt_soft
    return outs


def _row_chunks(ref, n_rows, lanes=slice(None)):
    return [lambda c=c: ref[c * TQ:(c + 1) * TQ, lanes] for c in range(n_rows // TQ)]


def _col_chunks(ref, n_cols, rows):
    return [lambda c=c: ref[rows, c * TQ:(c + 1) * TQ] for c in range(n_cols // TQ)]


def _half_query(q, half):
    first = lax.broadcasted_iota(jnp.int32, q.shape, 1) < HD
    return lambda: jnp.where(first if half == 0 else ~first, q, 0.0).astype(BF16)


def _pair_query(q):
    return lambda: jnp.concatenate([_half_query(q, 0)(), _half_query(q, 1)()], 0)


def _head_pair_rows(lo_t, hi_t):
    return jnp.concatenate([lo_t, hi_t], 0).T


def _pair_rows(o):
    tq = o.shape[1] // 2
    return _head_pair_rows(o[0:HD, 0:tq], o[HD:2 * HD, tq:2 * tq])


def _transpose_rows(eye_ref, x):
    return _dot_nt(eye_ref[...], x).astype(BF16)


def _rope_rows(x, cos, sin, rot_ref):
    return x * cos + _dot(x.astype(BF16), rot_ref[...]) * sin


def _attn_specs(n_batch, seq, ctx_len, q_width, kv_width):
    nq = seq // TQ
    ctx0 = n_batch * seq // ctx_len

    def q_idx(b, i):
        return (jnp.where(i < nq, b * nq + i, ctx0 + b), 0)

    return dict(
        q=lambda w: pl.BlockSpec((TQ, w), q_idx),
        lat=lambda w: pl.BlockSpec((seq, w), lambda b, i: (b, 0)),
        ctx=lambda w: pl.BlockSpec((ctx_len, w), lambda b, i: (ctx0 + b, 0)),
        tab=pl.BlockSpec((TQ, HB), lambda b, i: (i, 0)),
        full=lambda shape: pl.BlockSpec(shape, lambda b, i: (0,) * len(shape)),
    )


def _gqa_kernel(q_ref, kvl_ref, kvc_ref, cosq_ref, sinq_ref, cosk_ref, sink_ref, qg_ref, kg_ref,
                ones_ref, rot_ref, eye_ref, o_ref, kl_s, kc_s, vtl_s, vtc_s, *, nq):
    i = pl.program_id(1)

    def head_rms(x):
        ss = _dot((x * x).astype(BF16), ones_ref[...])
        return lax.rsqrt(ss * (1.0 / HD) + EPS)

    @pl.when(i == 0)
    def _():
        kl = kvl_ref[:, 0:HB].astype(F32)
        kn = kl * head_rms(kl) * kg_ref[...]
        kl_s[...] = _rope_rows(kn, cosk_ref[...], sink_ref[...], rot_ref).astype(BF16)
        kc = kvc_ref[:, 0:HB].astype(F32)
        kc_s[...] = (kc * head_rms(kc) * kg_ref[...]).astype(BF16)
        vtl_s[...] = _transpose_rows(eye_ref, kvl_ref[:, HB:2 * HB])
        vtc_s[...] = _transpose_rows(eye_ref, kvc_ref[:, HB:2 * HB])

    seq, ctx_len = kl_s.shape[0], kc_s.shape[0]

    def run(with_latent):
        keys = (_row_chunks(kl_s, seq) if with_latent else []) + _row_chunks(kc_s, ctx_len)
        tasks = []
        for blk in range(A_HEADS // 2):
            q = q_ref[:, blk * HB:(blk + 1) * HB].astype(F32)
            qn = q * head_rms(q) * qg_ref[...]
            qr = _rope_rows(qn, cosq_ref[...], sinq_ref[...], rot_ref) * (HD ** -0.5 * LOG2E)
            vts = (_col_chunks(vtl_s, seq, slice(None)) if with_latent else []) \
                + _col_chunks(vtc_s, ctx_len, slice(None))
            tasks.append((_pair_query(qr), keys, vts, None))
        outs = _attend_heads(tasks)
        for blk in range(A_HEADS // 2):
            o_ref[:, blk * HB:(blk + 1) * HB] = _pair_rows(outs[blk]).astype(BF16)

    @pl.when(i < nq)
    def _():
        run(True)

    @pl.when(i == nq)
    def _():
        run(False)


def _gqa(qa, kva, tabs, q_gain, k_gain, consts, seq, ctx_len, n_batch, with_ctx):
    nq = seq // TQ
    sp = _attn_specs(n_batch, seq, ctx_len, YW, 2 * HB)
    n_out = n_batch * (seq + ctx_len) if with_ctx else n_batch * seq
    return pl.pallas_call(
        functools.partial(_gqa_kernel, nq=nq),
        out_shape=jax.ShapeDtypeStruct((n_out, YW), BF16),
        grid=(n_batch, nq + int(with_ctx)),
        in_specs=[
            sp["q"](YW), sp["lat"](2 * HB), sp["ctx"](2 * HB), sp["tab"], sp["tab"],
            sp["full"]((seq, HB)), sp["full"]((seq, HB)),
            sp["full"]((1, HB)), sp["full"]((1, HB)),
            sp["full"]((HB, HB)), sp["full"]((HB, HB)), sp["full"]((HB, HB)),
        ],
        out_specs=sp["q"](YW),
        scratch_shapes=[pltpu.VMEM((seq, HB), BF16), pltpu.VMEM((ctx_len, HB), BF16),
                        pltpu.VMEM((HB, seq), BF16), pltpu.VMEM((HB, ctx_len), BF16)],
        compiler_params=_params(("arbitrary", "arbitrary")),
        name="gqa",
    )(qa, kva, kva, tabs["cos_a"], tabs["sin_a"], tabs["cos_a"], tabs["sin_a"], q_gain, k_gain,
      consts["ones_a"], consts["rot_a"], consts["eye"])


NA_TILE_ROWS = TQ // GRID_W
NA_BAND = NA_R + NA_TILE_ROWS
NA_KEYS = NA_BAND * GRID_W


def _na_band_start(i, rows):
    return jnp.clip(i * NA_TILE_ROWS - NA_R // 2, 0, rows - NA_BAND)


def _na_kernel(q_ref, kl_ref, vl_ref, kc_ref, vc_ref, bias_ref, eye_ref, o_ref, vtl_s, vtc_s, *, rows, nq):
    i = pl.program_id(1)
    scale = HD ** -0.5 * LOG2E
    band_tiles = NA_KEYS // TQ

    @pl.when(i == 0)
    def _():
        for blk in range(3):
            lanes = slice(blk * HB, (blk + 1) * HB)
            for t in range(nq):
                vtl_s[t, lanes, :] = _transpose_rows(eye_ref, vl_ref[t * TQ:(t + 1) * TQ, lanes])
            vtc_s[lanes, :] = _transpose_rows(eye_ref, vc_ref[:, lanes])

    def run(t0):
        tasks = []
        for blk in range(3):
            paired = 2 * blk + 1 < B_HEADS
            lanes = slice(blk * HB, (blk + 1) * HB)
            hrows = lanes if paired else slice(blk * HB, blk * HB + HD)
            q = q_ref[:, lanes].astype(F32) * scale
            keys = [lambda lanes=lanes: kc_ref[:, lanes]]
            vts = [lambda hrows=hrows: vtc_s[hrows, :]]
            biases = None
            if t0 is not None:
                keys = [lambda j=j, lanes=lanes: kl_ref[pl.ds(pl.multiple_of((t0 + j) * TQ, TQ), TQ), lanes]
                        for j in range(band_tiles)] + keys
                vts = [lambda j=j, hrows=hrows: vtl_s[t0 + j, hrows, :] for j in range(band_tiles)] + vts

                def bias_chunk(j, blk=blk, paired=paired):
                    rows = slice(j * TQ, (j + 1) * TQ)
                    if not paired:
                        return bias_ref[2 * blk, rows, :]
                    return jnp.concatenate([bias_ref[2 * blk, rows, :], bias_ref[2 * blk + 1, rows, :]], 1)

                biases = [functools.partial(bias_chunk, j) for j in range(band_tiles)] + [None]
            tasks.append((_pair_query(q) if paired else _half_query(q, 0), keys, vts, biases))
        outs = _attend_heads(tasks)
        for blk in range(3):
            o = outs[blk]
            y = _pair_rows(o) if o.shape[0] == HB else _head_pair_rows(o, jnp.zeros_like(o))
            o_ref[:, blk * HB:(blk + 1) * HB] = y.astype(BF16)

    @pl.when(i < nq)
    def _():
        run(_na_band_start(i, rows) // NA_TILE_ROWS)

    @pl.when(i == nq)
    def _():
        run(None)


def _na_tile_config(i, nq):
    return jnp.where(i == 0, 0, jnp.where(i >= nq - 1, 2, 1))


def _na(qb, kb, vb, bias, eye, seq, ctx_len, n_batch, with_ctx):
    nq = seq // TQ
    sp = _attn_specs(n_batch, seq, ctx_len, YW, YW)
    n_out = n_batch * (seq + ctx_len) if with_ctx else n_batch * seq
    bias_spec = pl.BlockSpec((None,) + bias.shape[1:], lambda b, i: (_na_tile_config(i, nq), 0, 0, 0))
    return pl.pallas_call(
        functools.partial(_na_kernel, rows=seq // GRID_W, nq=nq),
        out_shape=jax.ShapeDtypeStruct((n_out, YW), BF16),
        grid=(n_batch, nq + int(with_ctx)),
        in_specs=[sp["q"](YW), sp["lat"](YW), sp["lat"](YW), sp["ctx"](YW), sp["ctx"](YW), bias_spec,
                  sp["full"]((HB, HB))],
        out_specs=sp["q"](YW),
        scratch_shapes=[pltpu.VMEM((nq, YW, TQ), BF16), pltpu.VMEM((YW, ctx_len), BF16)],
        compiler_params=_params(("arbitrary", "arbitrary")),
        name="na",
    )(qb, kb, vb, kb, vb, bias, eye)


def _na_window_pattern(i, rows):
    r = i * NA_TILE_ROWS + np.arange(NA_TILE_ROWS)
    r0 = np.clip(r - NA_R // 2, 0, rows - NA_R)
    krow = int(np.clip(i * NA_TILE_ROWS - NA_R // 2, 0, rows - NA_BAND)) + np.arange(NA_BAND)
    valid = (krow[None, :] >= r0[:, None]) & (krow[None, :] < r0[:, None] + NA_R)
    dr = np.clip(krow[None, :] - r[:, None] + (NA_R - 1), 0, 2 * NA_R - 2)
    return valid, dr


def _na_bias(rpb, rows, nq):
    col = jnp.arange(GRID_W, dtype=jnp.int32)
    c0 = jnp.clip(col - NA_C // 2, 0, GRID_W - NA_C)
    col_in = (col[None, :] >= c0[:, None]) & (col[None, :] < c0[:, None] + NA_C)
    dc = jnp.clip(col[None, :] - col[:, None] + (NA_C - 1), 0, 2 * NA_C - 2)
    tbl = jnp.where(col_in[None, None], rpb[:, :, dc] * LOG2E, NEG_INF)
    patterns = [_na_window_pattern(i, rows) for i in range(nq)]
    for i in range(2, nq - 1):
        assert all(np.array_equal(a, b) for a, b in zip(patterns[i], patterns[1]))
    out = []
    for valid, dr in (patterns[0], patterns[1], patterns[nq - 1]):
        band = jnp.where(jnp.asarray(valid)[None, :, :, None, None], tbl[:, jnp.asarray(dr)], NEG_INF)
        out.append(band.transpose(0, 2, 4, 1, 3).reshape(B_HEADS, NA_KEYS, TQ))
    return jnp.stack(out).astype(F32)


C_QW = C_HEADS * HB


def _mla_kernel(cq_ref, kvl_ref, kvc_ref, cosq_ref, sinq_ref, cosk_ref, sink_ref, qg_ref, kg_ref,
                wuq_ref, wuk_ref, wuvt_ref, place_ref, rotq_ref, rotk_ref, o_ref, kl_s, kc_s, vl_s, vc_s, *, nq):
    i = pl.program_id(1)

    def latent_rms(x, gain):
        ms = jnp.mean(x * x, -1, keepdims=True)
        return (x * lax.rsqrt(ms + EPS) * gain).astype(BF16)

    @pl.when(i == 0)
    def _():
        cl = latent_rms(kvl_ref[:, 0:C_KVR].astype(F32), kg_ref[...])
        rl = _rope_rows(kvl_ref[:, C_KVR:2 * C_KVR].astype(F32), cosk_ref[...], sink_ref[...], rotk_ref)
        kl_s[...] = (_dot(cl, wuk_ref[...]) + _dot(rl.astype(BF16), place_ref[...])).astype(BF16)
        vl_s[...] = _dot_nt(wuvt_ref[...], cl).astype(BF16)
        cc = latent_rms(kvc_ref[:, 0:C_KVR].astype(F32), kg_ref[...])
        kc_s[...] = (_dot(cc, wuk_ref[...]) + _dot(kvc_ref[:, C_KVR:2 * C_KVR], place_ref[...])).astype(BF16)
        vc_s[...] = _dot_nt(wuvt_ref[...], cc).astype(BF16)

    seq, ctx_len = kl_s.shape[0], kc_s.shape[0]

    def run(with_latent):
        cq = latent_rms(cq_ref[...].astype(F32), qg_ref[...])
        tasks = []
        for h in range(C_HEADS):
            lanes = slice(h * HB, (h + 1) * HB)
            q = _dot(cq, wuq_ref[:, lanes])
            qr = _rope_rows(q, cosq_ref[...], sinq_ref[...], rotq_ref) * ((C_NOPE + C_ROPE) ** -0.5 * LOG2E)
            hrows = slice(h * C_V, (h + 1) * C_V)
            keys, vts = _row_chunks(kc_s, ctx_len, lanes), _col_chunks(vc_s, ctx_len, hrows)
            if with_latent:
                keys, vts = _row_chunks(kl_s, seq, lanes) + keys, _col_chunks(vl_s, seq, hrows) + vts
            tasks.append((lambda qr=qr: qr.astype(BF16), keys, vts, None))
        outs = _attend_heads(tasks)
        outs = [jnp.zeros_like(outs[0])] + outs
        for blk in range(3):
            o_ref[:, blk * HB:(blk + 1) * HB] = _head_pair_rows(outs[2 * blk], outs[2 * blk + 1]).astype(BF16)

    @pl.when(i < nq)
    def _():
        run(True)

    @pl.when(i == nq)
    def _():
        run(False)


def _mla(cq, ckv, tabs, q_gain, kv_gain, wts, consts, seq, ctx_len, n_batch, with_ctx):
    nq = seq // TQ
    sp = _attn_specs(n_batch, seq, ctx_len, C_QR, 2 * C_KVR)
    full = sp["full"]
    n_out = n_batch * (seq + ctx_len) if with_ctx else n_batch * seq
    return pl.pallas_call(
        functools.partial(_mla_kernel, nq=nq),
        out_shape=jax.ShapeDtypeStruct((n_out, YW), BF16),
        grid=(n_batch, nq + int(with_ctx)),
        in_specs=[
            sp["q"](C_QR), sp["lat"](2 * C_KVR), sp["ctx"](2 * C_KVR), sp["tab"], sp["tab"],
            full((seq, HB)), full((seq, HB)), full((1, C_QR)), full((1, C_KVR)),
            full((C_QR, C_QW)), full((C_KVR, C_QW)), full((C_HEADS * C_V, C_KVR)),
            full((HB, C_QW)), full((HB, HB)), full((HB, HB)),
        ],
        out_specs=sp["q"](YW),
        scratch_shapes=[pltpu.VMEM((seq, C_QW), BF16), pltpu.VMEM((ctx_len, C_QW), BF16),
                        pltpu.VMEM((C_HEADS * C_V, seq), BF16), pltpu.VMEM((C_HEADS * C_V, ctx_len), BF16)],
        compiler_params=_params(("arbitrary", "arbitrary")),
        name="mla",
    )(cq, ckv, ckv, tabs["cos_cq"], tabs["sin_cq"], tabs["cos_ck"], tabs["sin_ck"], q_gain, kv_gain,
      wts["wuq"], wts["wuk"], wts["wuvt"], consts["place_c"], consts["rot_cq"], consts["rot_ck"])


def _outproj_kernel(ya_ref, yb_ref, yc_ref, x_ref, mod_ref, wa_ref, wb_ref, wc_ref, g_ref, b_ref,
                    wr_ref, br_ref, tri_ref, x1_ref, h2_ref, route_ref, cnt_ref, carry_s):
    i = pl.program_id(0)

    @pl.when(i == 0)
    def _():
        carry_s[...] = jnp.zeros_like(carry_s)

    y = _dot(ya_ref[...], wa_ref[...]) + _dot(yb_ref[...], wb_ref[...]) + _dot(yc_ref[...], wc_ref[...])
    x1 = _ln_rows(ALPHA * x_ref[...] + mod_ref[2:3, :] * y) * g_ref[...] + b_ref[...]
    x1_ref[...] = x1
    h2 = _ln_rows(x1) * (1.0 + mod_ref[4:5, :]) + mod_ref[3:4, :]
    h2_ref[...] = h2

    logit = _dot(h2.astype(BF16), wr_ref[...]) + br_ref[...]
    lane = lax.broadcasted_iota(jnp.int32, logit.shape, 1)
    big = jnp.int32(1 << 20)

    def row_max(mask):
        return jnp.max(jnp.where(mask, logit, NEG_INF), -1, keepdims=True)

    def first_lane(mask):
        return jnp.min(jnp.where(mask, lane, big), -1, keepdims=True)

    g_mask = lane < N_GROUPS
    g_max = row_max(g_mask)
    g_idx = first_lane(g_mask & (logit == g_max))
    g_w = 1.0 / jnp.sum(jnp.where(g_mask, jnp.exp(logit - g_max), 0.0), -1, keepdims=True)
    e_base = EPG + EPG * g_idx
    e_mask = (lane >= e_base) & (lane < e_base + EPG)
    m1 = row_max(e_mask)
    l1 = first_lane(e_mask & (logit == m1))
    e_mask2 = e_mask & (lane != l1)
    m2 = row_max(e_mask2)
    l2 = first_lane(e_mask2 & (logit == m2))
    t = jnp.exp(m2 - m1)
    w1 = g_w / (1.0 + t)
    w2 = g_w * t / (1.0 + t)
    i1, i2 = l1 - e_base, l2 - e_base
    lo, hi = jnp.minimum(i1, i2), jnp.maximum(i1, i2)
    pair = lax.shift_right_logical(lo * (2 * EPG - 1 - lo), 1) + hi - lo - 1
    cls = g_idx * N_PAIR + pair
    w_lo = jnp.where(i1 < i2, w1, w2)
    w_hi = jnp.where(i1 < i2, w2, w1)

    onehot = (lane == cls).astype(F32)
    prefix = _dot(tri_ref[...], onehot.astype(BF16)) + carry_s[0:1, :]
    rank = jnp.sum(onehot * prefix, -1, keepdims=True)
    carry_s[...] = carry_s[...] + jnp.sum(onehot, 0, keepdims=True)
    cnt_ref[...] = carry_s[...]

    route_ref[...] = jnp.where(lane == 0, cls.astype(F32),
                               jnp.where(lane == 1, rank,
                                         jnp.where(lane == 2, w_lo, jnp.where(lane == 3, w_hi, 0.0))))


def _outproj(ya, yb, yc, xp, mod, w_parts, ln_g, ln_b, w_r, b_r, tri, n_rows, seq, n_lat, n_batch):
    row = lambda w: pl.BlockSpec((TM_OUT, w), lambda i: (i, 0))
    full = lambda shape: pl.BlockSpec(shape, lambda i: (0, 0))
    return pl.pallas_call(
        _outproj_kernel,
        out_shape=[
            jax.ShapeDtypeStruct((n_rows, D), F32),
            jax.ShapeDtypeStruct((n_rows, D), F32),
            jax.ShapeDtypeStruct((n_rows, LANE), F32),
            jax.ShapeDtypeStruct((8, LANE), F32),
        ],
        grid=(n_rows // TM_OUT,),
        in_specs=[
            row(YW), row(YW), row(YW), row(D),
            pl.BlockSpec((None, N_MOD, D), _mod_index(TM_OUT, seq, n_lat, n_batch)),
            full((YW, D)), full((YW, D)), full((YW, D)),
            full((1, D)), full((1, D)), full((D, LANE)), full((1, LANE)), full((TM_OUT, TM_OUT)),
        ],
        out_specs=[row(D), row(D), row(LANE), full((8, LANE))],
        scratch_shapes=[pltpu.VMEM((8, LANE), F32)],
        compiler_params=_params(("arbitrary",)),
        name="outproj",
    )(ya, yb, yc, xp, mod, *w_parts, ln_g, ln_b, w_r, b_r, tri)


ROW_UNROLL = 8


SUB = 8


def _permute_rows(starts_ref, cls_hbm, rank_hbm, idx_s, sem_idx, make_copy):
    i, n = pl.program_id(0), pl.num_programs(0)

    def index_copies(step, slot):
        return [pltpu.make_async_copy(src.at[step], idx_s[2 * slot + k], sem_idx.at[slot, k])
                for k, src in enumerate((cls_hbm, rank_hbm))]

    @pl.when(i == 0)
    def _():
        for cp in index_copies(0, 0):
            cp.start()

    for slot in range(2):

        @pl.when(lax.rem(i, 2) == slot)
        def _(slot=slot):
            for cp in index_copies(i, slot):
                cp.wait()

            @pl.when(i + 1 < n)
            def _():
                for cp in index_copies(i + 1, 1 - slot):
                    cp.start()

            cls_s, rank_s = idx_s[2 * slot], idx_s[2 * slot + 1]

            def start(u, c):
                for k in range(SUB):
                    t = u * SUB + k
                    make_copy(u, k, starts_ref[cls_s[t]] + rank_s[t]).start(priority=k % 2)
                return c

            def wait(u, c):
                for k in range(SUB):
                    make_copy(0, 0, 0).wait()
                return c

            lax.fori_loop(0, T_ROW // SUB, start, 0)
            lax.fori_loop(0, T_ROW // SUB, wait, 0)


def _dispatch_kernel(starts_ref, cls_hbm, rank_hbm, h_ref, hs_in, hs_out, *scratch):
    del hs_in
    idx_s, sem_idx, sem_rows = scratch[:4], scratch[4], scratch[5]
    _permute_rows(starts_ref, cls_hbm, rank_hbm, idx_s, sem_idx,
                  lambda u, k, pos: pltpu.make_async_copy(h_ref.at[u, pl.ds(k, 1)], hs_out.at[pl.ds(pos, 1)],
                                                          sem_rows))


_ROW_IDX_SCRATCH = [pltpu.SMEM((T_ROW,), jnp.int32)] * 4 + [pltpu.SemaphoreType.DMA((2, 2))]


def _dispatch(starts, cls_i, rank_i, h2, hs0):
    any_spec = pl.BlockSpec(memory_space=pl.ANY)
    return pl.pallas_call(
        _dispatch_kernel,
        out_shape=jax.ShapeDtypeStruct(hs0.shape, hs0.dtype),
        grid_spec=pltpu.PrefetchScalarGridSpec(
            num_scalar_prefetch=1,
            grid=(cls_i.shape[0],),
            in_specs=[any_spec, any_spec, pl.BlockSpec((T_ROW // SUB, SUB, D), lambda i, st: (i, 0, 0)), any_spec],
            out_specs=any_spec,
            scratch_shapes=_ROW_IDX_SCRATCH + [pltpu.SemaphoreType.DMA],
        ),
        input_output_aliases={4: 0},
        compiler_params=_params(("arbitrary",)),
        name="dispatch",
    )(starts, cls_i, rank_i, h2.reshape(-1, SUB, D), hs0)


def _moe_kernel(elo_ref, ehi_ref, valid_ref, h_ref, w1a, w3a, w2a, w1b, w3b, w2b, o_ref):
    j = pl.program_id(0)

    def ffn_bits(h, w1, w3, w2):
        hid = (_silu(_dot(h, w1[...])) * _dot(h, w3[...])).astype(BF16)
        y = _dot(hid, w2[...])
        return lax.bitcast_convert_type(y.astype(BF16).astype(F32), jnp.uint32)

    @pl.when(valid_ref[j] != 0)
    def _():
        h = h_ref[...].astype(BF16)
        y_lo = ffn_bits(h, w1a, w3a, w2a)
        y_hi = ffn_bits(h, w1b, w3b, w2b)
        o_ref[...] = lax.shift_right_logical(y_lo, jnp.uint32(16)) | (y_hi & jnp.uint32(0xFFFF0000))

    @pl.when(valid_ref[j] == 0)
    def _():
        o_ref[...] = jnp.zeros_like(o_ref)


def _moe(tile_elo, tile_ehi, tile_valid, hs, w1, w3, w2):
    lo_spec = lambda shape: pl.BlockSpec((None,) + shape, lambda j, elo, ehi, v: (elo[j], 0, 0))
    hi_spec = lambda shape: pl.BlockSpec((None,) + shape, lambda j, elo, ehi, v: (ehi[j], 0, 0))
    up, down = (D, D_EXP), (D_EXP, D)
    return pl.pallas_call(
        _moe_kernel,
        out_shape=jax.ShapeDtypeStruct((hs.shape[0], D), jnp.uint32),
        grid_spec=pltpu.PrefetchScalarGridSpec(
            num_scalar_prefetch=3,
            grid=(hs.shape[0] // TM_MOE,),
            in_specs=[
                pl.BlockSpec((TM_MOE, D), lambda j, elo, ehi, v: (j, 0)),
                lo_spec(up), lo_spec(up), lo_spec(down), hi_spec(up), hi_spec(up), hi_spec(down),
            ],
            out_specs=pl.BlockSpec((TM_MOE, D), lambda j, elo, ehi, v: (j, 0)),
        ),
        compiler_params=_params(("arbitrary",)),
        name="moe_ffn",
    )(tile_elo, tile_ehi, tile_valid, hs, w1, w3, w2, w1, w3, w2)


def _combine_kernel(starts_ref, cls_hbm, rank_hbm, ys_hbm, x1_ref, route_ref, mod_ref, g_ref, b_ref, o_ref,
                    *scratch):
    idx_s, sem_idx, buf, sem_rows = scratch[:4], scratch[4], scratch[5], scratch[6]
    _permute_rows(starts_ref, cls_hbm, rank_hbm, idx_s, sem_idx,
                  lambda u, k, pos: pltpu.make_async_copy(ys_hbm.at[pl.ds(pos, 1)], buf.at[u, pl.ds(k, 1)],
                                                          sem_rows))

    packed = buf[...].reshape(T_ROW, D)
    y_lo = lax.bitcast_convert_type(lax.shift_left(packed, jnp.uint32(16)), F32)
    y_hi = lax.bitcast_convert_type(packed & jnp.uint32(0xFFFF0000), F32)
    moe = route_ref[:, 2:3] * y_lo + route_ref[:, 3:4] * y_hi
    o_ref[...] = _ln_rows(ALPHA * x1_ref[...] + mod_ref[5:6, :] * moe) * g_ref[...] + b_ref[...]


def _combine(starts, cls_i, rank_i, ys, x1, route, mod, ln_g, ln_b, seq, n_lat, n_batch):
    n_rows = x1.shape[0]
    any_spec = pl.BlockSpec(memory_space=pl.ANY)
    mod_idx = _mod_index(T_ROW, seq, n_lat, n_batch)
    return pl.pallas_call(
        _combine_kernel,
        out_shape=jax.ShapeDtypeStruct((n_rows, D), F32),
        grid_spec=pltpu.PrefetchScalarGridSpec(
            num_scalar_prefetch=1,
            grid=(n_rows // T_ROW,),
            in_specs=[
                any_spec, any_spec, any_spec,
                pl.BlockSpec((T_ROW, D), lambda i, st: (i, 0)),
                pl.BlockSpec((T_ROW, LANE), lambda i, st: (i, 0)),
                pl.BlockSpec((None, N_MOD, D), lambda i, st: mod_idx(i)),
                pl.BlockSpec((1, D), lambda i, st: (0, 0)), pl.BlockSpec((1, D), lambda i, st: (0, 0)),
            ],
            out_specs=pl.BlockSpec((T_ROW, D), lambda i, st: (i, 0)),
            scratch_shapes=_ROW_IDX_SCRATCH + [pltpu.VMEM((T_ROW // SUB, SUB, D), jnp.uint32),
                                               pltpu.SemaphoreType.DMA],
        ),
        compiler_params=_params(("arbitrary",)),
        name="combine",
    )(starts, cls_i, rank_i, ys, x1, route, mod, ln_g, ln_b)


def _rot_matrix(width, start, half):
    r = np.zeros((width, width), np.float32)
    for j in range(half):
        r[start + half + j, start + j] = -1.0
        r[start + j, start + half + j] = 1.0
    return r


def _constants():
    ones_a = np.kron(np.eye(2, dtype=np.float32), np.ones((HD, HD), np.float32))
    rot_a = _rot_matrix(HB, 0, HD // 2) + _rot_matrix(HB, HD, HD // 2)
    rot_cq = _rot_matrix(HB, C_NOPE, C_ROPE // 2)
    rot_ck = _rot_matrix(HB, 0, C_ROPE // 2)
    place = np.zeros((HB, C_QW), np.float32)
    for h in range(C_HEADS):
        for j in range(C_ROPE):
            place[j, h * HB + C_NOPE + j] = 1.0
    tri = np.tril(np.ones((TM_OUT, TM_OUT), np.float32), -1)
    cls_lo, cls_hi = [], []
    for g in range(N_GROUPS):
        for lo in range(EPG):
            for hi in range(lo + 1, EPG):
                cls_lo.append(g * EPG + lo)
                cls_hi.append(g * EPG + hi)
    as_bf = lambda a: jnp.asarray(a, BF16)
    return dict(ones_a=as_bf(ones_a), rot_a=as_bf(rot_a), rot_cq=as_bf(rot_cq), rot_ck=as_bf(rot_ck),
                place_c=as_bf(place), tri=as_bf(tri), eye=as_bf(np.eye(HB, dtype=np.float32)),
                cls_lo=jnp.asarray(cls_lo, jnp.int32), cls_hi=jnp.asarray(cls_hi, jnp.int32))


def _rope_tables(seq):
    t = jnp.arange(seq, dtype=jnp.int32)
    row = (t // GRID_W).astype(F32)
    col = (t % GRID_W).astype(F32)

    def angles(dim):
        n_freq = dim // 4
        inv = THETA ** (-jnp.arange(n_freq, dtype=F32) / n_freq)
        ang = jnp.concatenate([row[:, None] * inv, col[:, None] * inv], -1)
        return jnp.concatenate([ang, jnp.zeros((TQ, dim // 2), F32)], 0)

    ang_a, ang_c = angles(HD), angles(C_ROPE)
    ones = lambda w: jnp.ones((seq + TQ, w), F32)
    zeros = lambda w: jnp.zeros((seq + TQ, w), F32)
    ca, sa = jnp.cos(ang_a), jnp.sin(ang_a)
    cc, sc = jnp.cos(ang_c), jnp.sin(ang_c)
    return dict(
        cos_a=jnp.tile(ca, (1, 4)), sin_a=jnp.tile(sa, (1, 4)),
        cos_cq=jnp.concatenate([ones(C_NOPE), cc, cc, ones(HB - C_NOPE - C_ROPE)], -1),
        sin_cq=jnp.concatenate([zeros(C_NOPE), sc, sc, zeros(HB - C_NOPE - C_ROPE)], -1),
        cos_ck=jnp.concatenate([cc, cc, ones(HB - C_ROPE)], -1),
        sin_ck=jnp.concatenate([sc, sc, zeros(HB - C_ROPE)], -1),
    )


def _pad_cols(w, width):
    return jnp.pad(w, ((0, 0), (0, width - w.shape[1])))


def _layer_weights(w_in, w_uq, w_ukv, w_out, w_rg, b_rg, w_re, b_re):
    pa = (A_HEADS + 2 * A_KV) * HD
    pb = 3 * B_HEADS * HD
    bw = B_HEADS * HD
    g = A_HEADS // A_KV
    qa = w_in[:, :A_HEADS * HD].reshape(D, A_KV, g, HD).transpose(0, 2, 1, 3).reshape(D, A_HEADS * HD)
    kva = w_in[:, A_HEADS * HD:pa]
    qb, kb, vb = (_pad_cols(w_in[:, pa + k * bw:pa + (k + 1) * bw], YW) for k in range(3))
    cq = w_in[:, pa + pb:pa + pb + C_QR]
    ckv = _pad_cols(w_in[:, pa + pb + C_QR:], 2 * C_KVR)
    w_all = jnp.concatenate([qa, kva, qb, kb, vb, cq, ckv], -1).astype(BF16)

    wuq = jnp.pad(w_uq.reshape(C_QR, C_HEADS, C_NOPE + C_ROPE), ((0, 0), (0, 0), (0, HB - C_NOPE - C_ROPE)))
    wuq = wuq.reshape(C_QR, C_QW).astype(BF16)
    ukv = w_ukv.reshape(C_KVR, C_HEADS, C_NOPE + C_V)
    wuk = jnp.pad(ukv[:, :, :C_NOPE], ((0, 0), (0, 0), (0, HB - C_NOPE))).reshape(C_KVR, C_QW).astype(BF16)
    wuvt = ukv[:, :, C_NOPE:].reshape(C_KVR, C_HEADS * C_V).T.astype(BF16)

    oa = w_out[:A_HEADS * HD].reshape(A_KV, g, HD, D).transpose(1, 0, 2, 3).reshape(A_HEADS * HD, D)
    ob = jnp.pad(w_out[A_HEADS * HD:A_HEADS * HD + bw], ((0, YW - bw), (0, 0)))
    oc = jnp.pad(w_out[A_HEADS * HD + bw:], ((C_V, 0), (0, 0)))
    w_parts = tuple(w.astype(BF16) for w in (oa, ob, oc))

    w_r = jnp.zeros((D, LANE), F32).at[:, :N_GROUPS].set(w_rg).at[:, EPG:EPG + N_EXP].set(w_re).astype(BF16)
    b_r = jnp.zeros((1, LANE), F32).at[0, :N_GROUPS].set(b_rg).at[0, EPG:EPG + N_EXP].set(b_re)
    return w_all, dict(wuq=wuq, wuk=wuk, wuvt=wuvt), w_parts, w_r, b_r


def kernel(x, c, ctx, c_ctx, w_mod, b_mod, w_in, q_gain_a, k_gain_a, rpb_b, q_lat_gain, kv_lat_gain,
           w_uq, w_ukv, w_out, ln1_g, ln1_b, w_rg, b_rg, w_re, b_re, w1, w3, w2, ln2_g, ln2_b):
    n_batch, seq, _ = x.shape
    ctx_len = ctx.shape[1]
    n_lat, n_ctx = n_batch * seq, n_batch * ctx_len
    assert ctx_len == TQ and seq % TM_IN == 0 and n_ctx % T_ROW == 0 and seq // GRID_W >= NA_BAND

    consts = _constants()
    tabs = _rope_tables(seq)
    mod_rows = -(-(n_batch + 1) // 8) * 8
    cc = jnp.zeros((mod_rows, D), F32).at[:n_batch].set(c).at[n_batch].set(c_ctx)
    xp = jnp.concatenate([x.reshape(n_lat, D), ctx.reshape(n_ctx, D)], 0)

    for l in range(DEPTH):
        with_ctx = l < DEPTH - 1
        mod = _modulation(cc, w_mod[l], b_mod[l]).reshape(mod_rows, N_MOD, D)
        w_all, mla_w, w_parts, w_r, b_r = _layer_weights(w_in[l], w_uq[l], w_ukv[l], w_out[l],
                                                        w_rg[l], b_rg[l], w_re[l], b_re[l])
        qa, kva, qb, kb, vb, cq, ckv = _inproj(xp, mod, w_all, seq, n_lat, n_batch)
        ya = _gqa(qa, kva, tabs, jnp.tile(q_gain_a[l], 2)[None], jnp.tile(k_gain_a[l], 2)[None],
                  consts, seq, ctx_len, n_batch, with_ctx)
        yb = _na(qb, kb, vb, _na_bias(rpb_b[l], seq // GRID_W, seq // TQ), consts["eye"],
                 seq, ctx_len, n_batch, with_ctx)
        yc = _mla(cq, ckv, tabs, q_lat_gain[l][None], kv_lat_gain[l][None], mla_w, consts,
                  seq, ctx_len, n_batch, with_ctx)

        n_rows = n_lat + n_ctx if with_ctx else n_lat
        x1, h2, route, counts = _outproj(ya, yb, yc, xp, mod, w_parts, ln1_g[l][None], ln1_b[l][None],
                                         w_r, b_r, consts["tri"], n_rows, seq, n_lat, n_batch)

        cnt = counts[0, :N_CLS].astype(jnp.int32)
        padded = (cnt + TM_MOE - 1) // TM_MOE * TM_MOE
        ends = jnp.cumsum(padded)
        starts = ends - padded
        n_tiles = n_rows // TM_MOE + N_CLS
        tile_start = jnp.arange(n_tiles, dtype=jnp.int32) * TM_MOE
        tile_cls = jnp.sum((ends[None, :] <= tile_start[:, None]).astype(jnp.int32), -1)
        tile_cls = jnp.minimum(tile_cls, N_CLS - 1)
        tile_valid = (tile_start < ends[-1]).astype(jnp.int32)
        cls_i = route[:, 0].astype(jnp.int32).reshape(n_rows // T_ROW, T_ROW)
        rank_i = route[:, 1].astype(jnp.int32).reshape(n_rows // T_ROW, T_ROW)

        hs = _dispatch(starts, cls_i, rank_i, h2, jnp.zeros((n_tiles * TM_MOE, D), F32))
        ys = _moe(consts["cls_lo"][tile_cls], consts["cls_hi"][tile_cls], tile_valid, hs,
                  w1[l].astype(BF16), w3[l].astype(BF16), w2[l].astype(BF16))
        xp = _combine(starts, cls_i, rank_i, ys, x1, route, mod, ln2_g[l][None], ln2_b[l][None],
                      seq, n_lat, n_batch)

    return xp.reshape(n_batch, seq, D)
```

```python
import functools

import numpy as np
import jax
import jax.numpy as jnp
from jax import lax
from jax.experimental import pallas as pl
from jax.experimental.pallas import tpu as pltpu

D = 1024
GRID_W = 64
HD = 64
A_HEADS, A_KV = 6, 2
B_HEADS = 5
NA_R, NA_C = 8, 16
C_HEADS, C_QR, C_KVR, C_NOPE, C_ROPE, C_V = 5, 256, 128, 64, 32, 64
THETA = 10000.0
N_GROUPS, EPG, N_EXP, D_EXP = 4, 8, 32, 256
N_PAIR = EPG * (EPG - 1) // 2
N_CLS = N_GROUPS * N_PAIR
DEPTH = 2
ALPHA = (2 * DEPTH) ** 0.25
EPS = 1e-6
N_MOD = 6

LANE = 128
HB = 2 * HD
YW = 3 * HB

TM_IN = 512
TM_OUT = 512
TQ = 256
TM_MOE = 256
T_ROW = 512
VMEM_LIMIT = 56 * 1024 * 1024

F32 = jnp.float32
BF16 = jnp.bfloat16
NEG_INF = float("-inf")
LOG2E = 1.4426950408889634


def _dot(a, b):
    return jnp.dot(a, b, preferred_element_type=F32)


def _dot_nt(a, b):
    return lax.dot_general(a, b, (((1,), (1,)), ((), ())), preferred_element_type=F32)


def _params(sem, **kw):
    return pltpu.CompilerParams(dimension_semantics=sem, vmem_limit_bytes=VMEM_LIMIT, **kw)


def _ln_rows(x):
    mu = jnp.mean(x, -1, keepdims=True)
    xc = x - mu
    var = jnp.mean(xc * xc, -1, keepdims=True)
    return xc * lax.rsqrt(var + EPS)


def _silu(a):
    return a * (1.0 / (1.0 + jnp.exp(-a)))


def _mod_kernel(c_ref, w_ref, b_ref, o_ref):
    s = _silu(c_ref[...])
    o_ref[...] = jnp.dot(s, w_ref[...], preferred_element_type=F32, precision=lax.Precision.HIGHEST) + b_ref[...]


def _modulation(cc, w_mod_l, b_mod_l):
    rows = cc.shape[0]
    tn = 1536
    return pl.pallas_call(
        _mod_kernel,
        out_shape=jax.ShapeDtypeStruct((rows, N_MOD * D), F32),
        grid=(N_MOD * D // tn,),
        in_specs=[
            pl.BlockSpec((rows, D), lambda j: (0, 0)),
            pl.BlockSpec((D, tn), lambda j: (0, j)),
            pl.BlockSpec((1, tn), lambda j: (0, j)),
        ],
        out_specs=pl.BlockSpec((rows, tn), lambda j: (0, j)),
        compiler_params=_params(("arbitrary",)),
        name="modulation",
    )(cc, w_mod_l, b_mod_l.reshape(1, -1))


IN_WIDTHS = (YW, 2 * HB, YW, YW, YW, C_QR, 2 * C_KVR)


def _for_row_source(x_refs, n_lat_tiles, body):
    if len(x_refs) == 1:
        body(x_refs[0])
        return
    i = pl.program_id(0)
    pl.when(i < n_lat_tiles)(lambda: body(x_refs[0]))
    pl.when(i >= n_lat_tiles)(lambda: body(x_refs[1]))


def _inproj_kernel(*refs, n_src, n_lat_tiles):
    x_refs, (mod_ref, w_ref), out_refs = refs[:n_src], refs[n_src:n_src + 2], refs[n_src + 2:]

    def body(x_ref):
        ln = _ln_rows(x_ref[...])
        h = (ln * (1.0 + mod_ref[1:2, :]) + mod_ref[0:1, :]).astype(BF16)
        p = _dot(h, w_ref[...])
        off = 0
        for o_ref, wd in zip(out_refs, IN_WIDTHS):
            o_ref[...] = p[:, off:off + wd].astype(BF16)
            off += wd

    _for_row_source(x_refs, n_lat_tiles, body)


def _mod_index(tile_rows, seq, n_lat, n_batch):
    tiles_per_batch = seq // tile_rows
    n_lat_tiles = n_lat // tile_rows
    return lambda i: (jnp.where(i < n_lat_tiles, i // tiles_per_batch, n_batch), 0, 0)


def _row_source_specs(xs, tile_rows, n_lat):
    if len(xs) == 1:
        return [pl.BlockSpec((tile_rows, D), lambda i: (i, 0))]
    nl = n_lat // tile_rows
    return [pl.BlockSpec((tile_rows, D), lambda i: (jnp.minimum(i, nl - 1), 0)),
            pl.BlockSpec((tile_rows, D), lambda i: (jnp.maximum(i - nl, 0), 0))]


def _inproj(xs, mod, w_all, seq, n_lat, n_batch):
    n = sum(x.shape[0] for x in xs)
    return pl.pallas_call(
        functools.partial(_inproj_kernel, n_src=len(xs), n_lat_tiles=n_lat // TM_IN),
        out_shape=[jax.ShapeDtypeStruct((n, wd), BF16) for wd in IN_WIDTHS],
        grid=(n // TM_IN,),
        in_specs=_row_source_specs(xs, TM_IN, n_lat) + [
            pl.BlockSpec((None, N_MOD, D), _mod_index(TM_IN, seq, n_lat, n_batch)),
            pl.BlockSpec((D, sum(IN_WIDTHS)), lambda i: (0, 0)),
        ],
        out_specs=[pl.BlockSpec((TM_IN, wd), lambda i: (i, 0)) for wd in IN_WIDTHS],
        compiler_params=_params(("arbitrary",)),
        name="inproj",
    )(*xs, mod, w_all)


KEY_CHUNK = 64


def _scores(task):
    qm, key_thunks, _, bias_thunks = task
    chunks = [k() for k in key_thunks]
    s = _dot_nt(jnp.concatenate(chunks, 0), qm())
    parts, r0 = [], 0
    for c, k in enumerate(chunks):
        part = s[r0:r0 + k.shape[0]]
        if bias_thunks is not None and bias_thunks[c] is not None:
            part = part + bias_thunks[c]()
        parts.append(part)
        r0 += k.shape[0]
    return parts


def _softmax(parts):
    blocks = [s[c:c + KEY_CHUNK] for s in parts for c in range(0, s.shape[0], KEY_CHUNK)]
    m = jnp.max(functools.reduce(jnp.maximum, blocks), 0, keepdims=True)
    m_b = jnp.broadcast_to(m, blocks[0].shape)
    acc, p_blocks = None, []
    for blk in blocks:
        p = jnp.exp2(blk - m_b)
        acc = p if acc is None else acc + p
        p_blocks.append(p.astype(BF16))
    return jnp.concatenate(p_blocks, 0), jnp.sum(acc, 0, keepdims=True)


def _attend_heads(tasks):
    outs, scored, soft = [], None, None
    for task in list(tasks) + [None, None]:
        nxt = None if task is None else (_scores(task), task[2])
        nxt_soft = None if scored is None else _softmax(scored[0]) + (scored[1],)
        if soft is not None:
            p, den, vt_thunks = soft
            outs.append(_dot(jnp.concatenate([v() for v in vt_thunks], 1), p) * (1.0 / den))
        scored, soft = nxt, nxt_soft
    return outs


def _row_chunks(ref, n_rows, lanes=slice(None)):
    return [lambda c=c: ref[c * TQ:(c + 1) * TQ, lanes] for c in range(n_rows // TQ)]


def _col_chunks(ref, n_cols, rows):
    return [lambda c=c: ref[rows, c * TQ:(c + 1) * TQ] for c in range(n_cols // TQ)]


def _half_query(q, half):
    first = lax.broadcasted_iota(jnp.int32, q.shape, 1) < HD
    return lambda: jnp.where(first if half == 0 else ~first, q, 0.0).astype(BF16)


def _pair_query(q):
    return lambda: jnp.concatenate([_half_query(q, 0)(), _half_query(q, 1)()], 0)


def _head_pair_rows(lo_t, hi_t):
    return jnp.concatenate([lo_t, hi_t], 0).T


def _pair_rows(o):
    tq = o.shape[1] // 2
    return _head_pair_rows(o[0:HD, 0:tq], o[HD:2 * HD, tq:2 * tq])


def _transpose_rows(eye_ref, x):
    return _dot_nt(eye_ref[...], x).astype(BF16)


def _rope_rows(x, cos, sin, rot_ref):
    return x * cos + _dot(x.astype(BF16), rot_ref[...]) * sin


def _attn_specs(n_batch, seq, ctx_len, q_width, kv_width):
    nq = seq // TQ
    ctx0 = n_batch * seq // ctx_len

    def q_idx(b, i):
        return (jnp.where(i < nq, b * nq + i, ctx0 + b), 0)

    return dict(
        q=lambda w: pl.BlockSpec((TQ, w), q_idx),
        lat=lambda w: pl.BlockSpec((seq, w), lambda b, i: (b, 0)),
        ctx=lambda w: pl.BlockSpec((ctx_len, w), lambda b, i: (ctx0 + b, 0)),
        tab=pl.BlockSpec((TQ, HB), lambda b, i: (i, 0)),
        full=lambda shape: pl.BlockSpec(shape, lambda b, i: (0,) * len(shape)),
    )


def _gqa_kernel(q_ref, kvl_ref, kvc_ref, cosq_ref, sinq_ref, cosk_ref, sink_ref, qg_ref, kg_ref,
                ones_ref, rot_ref, eye_ref, o_ref, kl_s, kc_s, vtl_s, vtc_s, *, nq):
    i = pl.program_id(1)

    def head_rms(x):
        ss = _dot((x * x).astype(BF16), ones_ref[...])
        return lax.rsqrt(ss * (1.0 / HD) + EPS)

    @pl.when(i == 0)
    def _():
        kl = kvl_ref[:, 0:HB].astype(F32)
        kn = kl * head_rms(kl) * kg_ref[...]
        kl_s[...] = _rope_rows(kn, cosk_ref[...], sink_ref[...], rot_ref).astype(BF16)
        kc = kvc_ref[:, 0:HB].astype(F32)
        kc_s[...] = (kc * head_rms(kc) * kg_ref[...]).astype(BF16)
        vtl_s[...] = _transpose_rows(eye_ref, kvl_ref[:, HB:2 * HB])
        vtc_s[...] = _transpose_rows(eye_ref, kvc_ref[:, HB:2 * HB])

    seq, ctx_len = kl_s.shape[0], kc_s.shape[0]

    def run(with_latent):
        keys = (_row_chunks(kl_s, seq) if with_latent else []) + _row_chunks(kc_s, ctx_len)
        tasks = []
        for blk in range(A_HEADS // 2):
            q = q_ref[:, blk * HB:(blk + 1) * HB].astype(F32)
            qn = q * head_rms(q) * qg_ref[...]
            qr = _rope_rows(qn, cosq_ref[...], sinq_ref[...], rot_ref) * (HD ** -0.5 * LOG2E)
            vts = (_col_chunks(vtl_s, seq, slice(None)) if with_latent else []) \
                + _col_chunks(vtc_s, ctx_len, slice(None))
            tasks.append((_pair_query(qr), keys, vts, None))
        outs = _attend_heads(tasks)
        for blk in range(A_HEADS // 2):
            o_ref[:, blk * HB:(blk + 1) * HB] = _pair_rows(outs[blk]).astype(BF16)

    @pl.when(i < nq)
    def _():
        run(True)

    @pl.when(i == nq)
    def _():
        run(False)


def _gqa(qa, kva, tabs, q_gain, k_gain, consts, seq, ctx_len, n_batch, with_ctx):
    nq = seq // TQ
    sp = _attn_specs(n_batch, seq, ctx_len, YW, 2 * HB)
    n_out = n_batch * (seq + ctx_len) if with_ctx else n_batch * seq
    return pl.pallas_call(
        functools.partial(_gqa_kernel, nq=nq),
        out_shape=jax.ShapeDtypeStruct((n_out, YW), BF16),
        grid=(n_batch, nq + int(with_ctx)),
        in_specs=[
            sp["q"](YW), sp["lat"](2 * HB), sp["ctx"](2 * HB), sp["tab"], sp["tab"],
            sp["full"]((seq, HB)), sp["full"]((seq, HB)),
            sp["full"]((1, HB)), sp["full"]((1, HB)),
            sp["full"]((HB, HB)), sp["full"]((HB, HB)), sp["full"]((HB, HB)),
        ],
        out_specs=sp["q"](YW),
        scratch_shapes=[pltpu.VMEM((seq, HB), BF16), pltpu.VMEM((ctx_len, HB), BF16),
                        pltpu.VMEM((HB, seq), BF16), pltpu.VMEM((HB, ctx_len), BF16)],
        compiler_params=_params(("arbitrary", "arbitrary")),
        name="gqa",
    )(qa, kva, kva, tabs["cos_a"], tabs["sin_a"], tabs["cos_a"], tabs["sin_a"], q_gain, k_gain,
      consts["ones_a"], consts["rot_a"], consts["eye"])


NA_TILE_ROWS = TQ // GRID_W
NA_BAND = NA_R + NA_TILE_ROWS
NA_KEYS = NA_BAND * GRID_W


def _na_band_start(i, rows):
    return jnp.clip(i * NA_TILE_ROWS - NA_R // 2, 0, rows - NA_BAND)


def _na_kernel(q_ref, kl_ref, vl_ref, kc_ref, vc_ref, bias_ref, eye_ref, o_ref, vtl_s, vtc_s, *, rows, nq):
    i = pl.program_id(1)
    scale = HD ** -0.5 * LOG2E
    band_tiles = NA_KEYS // TQ

    @pl.when(i == 0)
    def _():
        for blk in range(3):
            lanes = slice(blk * HB, (blk + 1) * HB)
            for t in range(nq):
                vtl_s[t, lanes, :] = _transpose_rows(eye_ref, vl_ref[t * TQ:(t + 1) * TQ, lanes])
            vtc_s[lanes, :] = _transpose_rows(eye_ref, vc_ref[:, lanes])

    def run(t0):
        tasks = []
        for blk in range(3):
            paired = 2 * blk + 1 < B_HEADS
            lanes = slice(blk * HB, (blk + 1) * HB)
            hrows = lanes if paired else slice(blk * HB, blk * HB + HD)
            q = q_ref[:, lanes].astype(F32) * scale
            keys = [lambda lanes=lanes: kc_ref[:, lanes]]
            vts = [lambda hrows=hrows: vtc_s[hrows, :]]
            biases = None
            if t0 is not None:
                keys = [lambda j=j, lanes=lanes: kl_ref[pl.ds(pl.multiple_of((t0 + j) * TQ, TQ), TQ), lanes]
                        for j in range(band_tiles)] + keys
                vts = [lambda j=j, hrows=hrows: vtl_s[t0 + j, hrows, :] for j in range(band_tiles)] + vts

                def bias_chunk(j, blk=blk, paired=paired):
                    rows = slice(j * TQ, (j + 1) * TQ)
                    if not paired:
                        return bias_ref[2 * blk, rows, :]
                    return jnp.concatenate([bias_ref[2 * blk, rows, :], bias_ref[2 * blk + 1, rows, :]], 1)

                biases = [functools.partial(bias_chunk, j) for j in range(band_tiles)] + [None]
            tasks.append((_pair_query(q) if paired else _half_query(q, 0), keys, vts, biases))
        outs = _attend_heads(tasks)
        for blk in range(3):
            o = outs[blk]
            y = _pair_rows(o) if o.shape[0] == HB else _head_pair_rows(o, jnp.zeros_like(o))
            o_ref[:, blk * HB:(blk + 1) * HB] = y.astype(BF16)

    @pl.when(i < nq)
    def _():
        run(_na_band_start(i, rows) // NA_TILE_ROWS)

    @pl.when(i == nq)
    def _():
        run(None)


def _na_tile_config(i, nq):
    return jnp.where(i == 0, 0, jnp.where(i >= nq - 1, 2, 1))


def _na(qb, kb, vb, bias, eye, seq, ctx_len, n_batch, with_ctx):
    nq = seq // TQ
    sp = _attn_specs(n_batch, seq, ctx_len, YW, YW)
    n_out = n_batch * (seq + ctx_len) if with_ctx else n_batch * seq
    bias_spec = pl.BlockSpec((None,) + bias.shape[1:], lambda b, i: (_na_tile_config(i, nq), 0, 0, 0))
    return pl.pallas_call(
        functools.partial(_na_kernel, rows=seq // GRID_W, nq=nq),
        out_shape=jax.ShapeDtypeStruct((n_out, YW), BF16),
        grid=(n_batch, nq + int(with_ctx)),
        in_specs=[sp["q"](YW), sp["lat"](YW), sp["lat"](YW), sp["ctx"](YW), sp["ctx"](YW), bias_spec,
                  sp["full"]((HB, HB))],
        out_specs=sp["q"](YW),
        scratch_shapes=[pltpu.VMEM((nq, YW, TQ), BF16), pltpu.VMEM((YW, ctx_len), BF16)],
        compiler_params=_params(("arbitrary", "arbitrary")),
        name="na",
    )(qb, kb, vb, kb, vb, bias, eye)


def _na_window_pattern(i, rows):
    r = i * NA_TILE_ROWS + np.arange(NA_TILE_ROWS)
    r0 = np.clip(r - NA_R // 2, 0, rows - NA_R)
    krow = int(np.clip(i * NA_TILE_ROWS - NA_R // 2, 0, rows - NA_BAND)) + np.arange(NA_BAND)
    valid = (krow[None, :] >= r0[:, None]) & (krow[None, :] < r0[:, None] + NA_R)
    dr = np.clip(krow[None, :] - r[:, None] + (NA_R - 1), 0, 2 * NA_R - 2)
    return valid, dr


def _na_bias(rpb, rows, nq):
    col = jnp.arange(GRID_W, dtype=jnp.int32)
    c0 = jnp.clip(col - NA_C // 2, 0, GRID_W - NA_C)
    col_in = (col[None, :] >= c0[:, None]) & (col[None, :] < c0[:, None] + NA_C)
    dc = jnp.clip(col[None, :] - col[:, None] + (NA_C - 1), 0, 2 * NA_C - 2)
    tbl = jnp.where(col_in[None, None], rpb[:, :, dc] * LOG2E, NEG_INF)
    patterns = [_na_window_pattern(i, rows) for i in range(nq)]
    for i in range(2, nq - 1):
        assert all(np.array_equal(a, b) for a, b in zip(patterns[i], patterns[1]))
    out = []
    for valid, dr in (patterns[0], patterns[1], patterns[nq - 1]):
        band = jnp.where(jnp.asarray(valid)[None, :, :, None, None], tbl[:, jnp.asarray(dr)], NEG_INF)
        out.append(band.transpose(0, 2, 4, 1, 3).reshape(B_HEADS, NA_KEYS, TQ))
    return jnp.stack(out).astype(F32)


C_QW = C_HEADS * HB


def _mla_kernel(cq_ref, kvl_ref, kvc_ref, cosq_ref, sinq_ref, cosk_ref, sink_ref, qg_ref, kg_ref,
                wuq_ref, wuk_ref, wuvt_ref, place_ref, rotq_ref, rotk_ref, o_ref, kl_s, kc_s, vl_s, vc_s, *, nq):
    i = pl.program_id(1)

    def latent_rms(x, gain):
        ms = jnp.mean(x * x, -1, keepdims=True)
        return (x * lax.rsqrt(ms + EPS) * gain).astype(BF16)

    @pl.when(i == 0)
    def _():
        cl = latent_rms(kvl_ref[:, 0:C_KVR].astype(F32), kg_ref[...])
        rl = _rope_rows(kvl_ref[:, C_KVR:2 * C_KVR].astype(F32), cosk_ref[...], sink_ref[...], rotk_ref)
        kl_s[...] = (_dot(cl, wuk_ref[...]) + _dot(rl.astype(BF16), place_ref[...])).astype(BF16)
        vl_s[...] = _dot_nt(wuvt_ref[...], cl).astype(BF16)
        cc = latent_rms(kvc_ref[:, 0:C_KVR].astype(F32), kg_ref[...])
        kc_s[...] = (_dot(cc, wuk_ref[...]) + _dot(kvc_ref[:, C_KVR:2 * C_KVR], place_ref[...])).astype(BF16)
        vc_s[...] = _dot_nt(wuvt_ref[...], cc).astype(BF16)

    seq, ctx_len = kl_s.shape[0], kc_s.shape[0]

    def run(with_latent):
        cq = latent_rms(cq_ref[...].astype(F32), qg_ref[...])
        tasks = []
        for h in range(C_HEADS):
            lanes = slice(h * HB, (h + 1) * HB)
            q = _dot(cq, wuq_ref[:, lanes])
            qr = _rope_rows(q, cosq_ref[...], sinq_ref[...], rotq_ref) * ((C_NOPE + C_ROPE) ** -0.5 * LOG2E)
            hrows = slice(h * C_V, (h + 1) * C_V)
            keys, vts = _row_chunks(kc_s, ctx_len, lanes), _col_chunks(vc_s, ctx_len, hrows)
            if with_latent:
                keys, vts = _row_chunks(kl_s, seq, lanes) + keys, _col_chunks(vl_s, seq, hrows) + vts
            tasks.append((lambda qr=qr: qr.astype(BF16), keys, vts, None))
        outs = _attend_heads(tasks)
        outs = [jnp.zeros_like(outs[0])] + outs
        for blk in range(3):
            o_ref[:, blk * HB:(blk + 1) * HB] = _head_pair_rows(outs[2 * blk], outs[2 * blk + 1]).astype(BF16)

    @pl.when(i < nq)
    def _():
        run(True)

    @pl.when(i == nq)
    def _():
        run(False)


def _mla(cq, ckv, tabs, q_gain, kv_gain, wts, consts, seq, ctx_len, n_batch, with_ctx):
    nq = seq // TQ
    sp = _attn_specs(n_batch, seq, ctx_len, C_QR, 2 * C_KVR)
    full = sp["full"]
    n_out = n_batch * (seq + ctx_len) if with_ctx else n_batch * seq
    return pl.pallas_call(
        functools.partial(_mla_kernel, nq=nq),
        out_shape=jax.ShapeDtypeStruct((n_out, YW), BF16),
        grid=(n_batch, nq + int(with_ctx)),
        in_specs=[
            sp["q"](C_QR), sp["lat"](2 * C_KVR), sp["ctx"](2 * C_KVR), sp["tab"], sp["tab"],
            full((seq, HB)), full((seq, HB)), full((1, C_QR)), full((1, C_KVR)),
            full((C_QR, C_QW)), full((C_KVR, C_QW)), full((C_HEADS * C_V, C_KVR)),
            full((HB, C_QW)), full((HB, HB)), full((HB, HB)),
        ],
        out_specs=sp["q"](YW),
        scratch_shapes=[pltpu.VMEM((seq, C_QW), BF16), pltpu.VMEM((ctx_len, C_QW), BF16),
                        pltpu.VMEM((C_HEADS * C_V, seq), BF16), pltpu.VMEM((C_HEADS * C_V, ctx_len), BF16)],
        compiler_params=_params(("arbitrary", "arbitrary")),
        name="mla",
    )(cq, ckv, ckv, tabs["cos_cq"], tabs["sin_cq"], tabs["cos_ck"], tabs["sin_ck"], q_gain, kv_gain,
      wts["wuq"], wts["wuk"], wts["wuvt"], consts["place_c"], consts["rot_cq"], consts["rot_ck"])


ROUTER_ROWS = (N_GROUPS + 1) * EPG


def _outproj_kernel(*refs, n_src, n_lat_tiles):
    x_refs = refs[:n_src]
    (ya_ref, yb_ref, yc_ref, mod_ref, wa_ref, wb_ref, wc_ref, g_ref, b_ref, wrt_ref, brt_ref, tri_ref,
     x1_ref, h2_ref, route_ref, idx_ref, cnt_ref, carry_s) = refs[n_src:]
    i = pl.program_id(0)

    @pl.when(i == 0)
    def _():
        carry_s[...] = jnp.zeros_like(carry_s)

    y = _dot(ya_ref[...], wa_ref[...]) + _dot(yb_ref[...], wb_ref[...]) + _dot(yc_ref[...], wc_ref[...])

    def residual(x_ref):
        x1_ref[...] = _ln_rows(ALPHA * x_ref[...] + mod_ref[2:3, :] * y) * g_ref[...] + b_ref[...]

    _for_row_source(x_refs, n_lat_tiles, residual)
    h2 = _ln_rows(x1_ref[...]) * (1.0 + mod_ref[4:5, :]) + mod_ref[3:4, :]
    h2_ref[...] = h2

    logit = _dot_nt(wrt_ref[...], h2.astype(BF16)) + brt_ref[:, 0:1]
    sub = lax.broadcasted_iota(jnp.int32, (EPG, TM_OUT), 0)

    def col_max(v):
        return jnp.max(v, 0, keepdims=True)

    def first_row(mask):
        return jnp.min(jnp.where(mask, sub, EPG), 0, keepdims=True)

    g_logit = jnp.where(sub < N_GROUPS, logit[0:EPG], NEG_INF)
    g_max = col_max(g_logit)
    g_idx = first_row(g_logit == g_max)
    g_w = 1.0 / jnp.sum(jnp.exp(g_logit - g_max), 0, keepdims=True)
    e_logit = logit[EPG:2 * EPG]
    for g in range(1, N_GROUPS):
        e_logit = jnp.where(g_idx == g, logit[(g + 1) * EPG:(g + 2) * EPG], e_logit)
    m1 = col_max(e_logit)
    i1 = first_row(e_logit == m1)
    rest = jnp.where(sub == i1, NEG_INF, e_logit)
    m2 = col_max(rest)
    i2 = first_row(rest == m2)
    t = jnp.exp(m2 - m1)
    w1 = g_w / (1.0 + t)
    w2 = g_w * t / (1.0 + t)
    lo, hi = jnp.minimum(i1, i2), jnp.maximum(i1, i2)
    pair = lax.shift_right_logical(lo * (2 * EPG - 1 - lo), jnp.ones_like(lo)) + hi - lo - 1
    cls = g_idx * N_PAIR + pair
    w_lo = jnp.where(i1 < i2, w1, w2)
    w_hi = jnp.where(i1 < i2, w2, w1)

    cls_row = lax.broadcasted_iota(jnp.int32, (LANE, TM_OUT), 0)
    onehot = (cls_row == cls).astype(F32)
    prefix = _dot(onehot.astype(BF16), tri_ref[...]) + carry_s[:, 0:1]
    rank = jnp.sum(onehot * prefix, 0, keepdims=True)
    carry_s[...] = carry_s[...] + jnp.sum(onehot, 1, keepdims=True)
    cnt_ref[...] = carry_s[...]

    idx_ref[...] = jnp.where(sub == 0, cls, jnp.where(sub == 1, rank.astype(jnp.int32), 0))
    rows = jnp.where(sub == 0, w_lo, jnp.where(sub == 1, w_hi, 0.0))
    route_ref[...] = jnp.concatenate([rows, jnp.zeros((LANE - EPG, TM_OUT), F32)], 0).T


def _outproj(ya, yb, yc, xs, mod, w_parts, ln_g, ln_b, w_rt, b_rt, tri, n_rows, seq, n_lat, n_batch):
    row = lambda w: pl.BlockSpec((TM_OUT, w), lambda i: (i, 0))
    full = lambda shape: pl.BlockSpec(shape, lambda i: (0, 0))
    n_tiles = n_rows // TM_OUT
    return pl.pallas_call(
        functools.partial(_outproj_kernel, n_src=len(xs), n_lat_tiles=n_lat // TM_OUT),
        out_shape=[
            jax.ShapeDtypeStruct((n_rows, D), F32),
            jax.ShapeDtypeStruct((n_rows, D), F32),
            jax.ShapeDtypeStruct((n_rows, LANE), F32),
            jax.ShapeDtypeStruct((n_tiles, EPG, TM_OUT), jnp.int32),
            jax.ShapeDtypeStruct((LANE, LANE), F32),
        ],
        grid=(n_tiles,),
        in_specs=_row_source_specs(xs, TM_OUT, n_lat) + [
            row(YW), row(YW), row(YW),
            pl.BlockSpec((None, N_MOD, D), _mod_index(TM_OUT, seq, n_lat, n_batch)),
            full((YW, D)), full((YW, D)), full((YW, D)),
            full((1, D)), full((1, D)), full((ROUTER_ROWS, D)), full((ROUTER_ROWS, LANE)), full((TM_OUT, TM_OUT)),
        ],
        out_specs=[row(D), row(D), row(LANE), pl.BlockSpec((None, EPG, TM_OUT), lambda i: (i, 0, 0)),
                   full((LANE, LANE))],
        scratch_shapes=[pltpu.VMEM((LANE, LANE), F32)],
        compiler_params=_params(("arbitrary",)),
        name="outproj",
    )(*xs, ya, yb, yc, mod, *w_parts, ln_g, ln_b, w_rt, b_rt, tri)


SUB = 8


def _permute_rows(starts_ref, idx_hbm, idx_s, sem_idx, make_copy):
    i, n = pl.program_id(0), pl.num_programs(0)

    def index_copies(step, slot):
        return [pltpu.make_async_copy(idx_hbm.at[step, k], idx_s[2 * slot + k], sem_idx.at[slot, k])
                for k in range(2)]

    @pl.when(i == 0)
    def _():
        for cp in index_copies(0, 0):
            cp.start()

    for slot in range(2):

        @pl.when(lax.rem(i, 2) == slot)
        def _(slot=slot):
            for cp in index_copies(i, slot):
                cp.wait()

            @pl.when(i + 1 < n)
            def _():
                for cp in index_copies(i + 1, 1 - slot):
                    cp.start()

            cls_s, rank_s = idx_s[2 * slot], idx_s[2 * slot + 1]

            def start(u, c):
                for k in range(SUB):
                    t = u * SUB + k
                    make_copy(u, k, starts_ref[cls_s[t]] + rank_s[t]).start(priority=k % 2)
                return c

            def wait(u, c):
                for k in range(SUB):
                    make_copy(0, 0, 0).wait()
                return c

            lax.fori_loop(0, T_ROW // SUB, start, 0)
            lax.fori_loop(0, T_ROW // SUB, wait, 0)


def _dispatch_kernel(starts_ref, idx_hbm, h_ref, hs_in, hs_out, *scratch):
    del hs_in
    idx_s, sem_idx, sem_rows = scratch[:4], scratch[4], scratch[5]
    _permute_rows(starts_ref, idx_hbm, idx_s, sem_idx,
                  lambda u, k, pos: pltpu.make_async_copy(h_ref.at[u, pl.ds(k, 1)], hs_out.at[pl.ds(pos, 1)],
                                                          sem_rows))


_ROW_IDX_SCRATCH = [pltpu.SMEM((T_ROW,), jnp.int32)] * 4 + [pltpu.SemaphoreType.DMA((2, 2))]


def _dispatch(starts, idx, h2, hs0):
    any_spec = pl.BlockSpec(memory_space=pl.ANY)
    return pl.pallas_call(
        _dispatch_kernel,
        out_shape=jax.ShapeDtypeStruct(hs0.shape, hs0.dtype),
        grid_spec=pltpu.PrefetchScalarGridSpec(
            num_scalar_prefetch=1,
            grid=(idx.shape[0],),
            in_specs=[any_spec, pl.BlockSpec((T_ROW // SUB, SUB, D), lambda i, st: (i, 0, 0)), any_spec],
            out_specs=any_spec,
            scratch_shapes=_ROW_IDX_SCRATCH + [pltpu.SemaphoreType.DMA],
        ),
        input_output_aliases={3: 0},
        compiler_params=_params(("arbitrary",)),
        name="dispatch",
    )(starts, idx, h2.reshape(-1, SUB, D), hs0)


def _moe_kernel(elo_ref, ehi_ref, valid_ref, h_ref, w1a, w3a, w2a, w1b, w3b, w2b, o_ref):
    j = pl.program_id(0)

    def ffn_bits(h, w1, w3, w2):
        hid = (_silu(_dot(h, w1[...])) * _dot(h, w3[...])).astype(BF16)
        y = _dot(hid, w2[...])
        return lax.bitcast_convert_type(y.astype(BF16).astype(F32), jnp.uint32)

    @pl.when(valid_ref[j] != 0)
    def _():
        h = h_ref[...].astype(BF16)
        y_lo = ffn_bits(h, w1a, w3a, w2a)
        y_hi = ffn_bits(h, w1b, w3b, w2b)
        o_ref[...] = lax.shift_right_logical(y_lo, jnp.uint32(16)) | (y_hi & jnp.uint32(0xFFFF0000))

    @pl.when(valid_ref[j] == 0)
    def _():
        o_ref[...] = jnp.zeros_like(o_ref)


def _moe(tile_elo, tile_ehi, tile_valid, hs, w1, w3, w2):
    lo_spec = lambda shape: pl.BlockSpec((None,) + shape, lambda j, elo, ehi, v: (elo[j], 0, 0))
    hi_spec = lambda shape: pl.BlockSpec((None,) + shape, lambda j, elo, ehi, v: (ehi[j], 0, 0))
    up, down = (D, D_EXP), (D_EXP, D)
    return pl.pallas_call(
        _moe_kernel,
        out_shape=jax.ShapeDtypeStruct((hs.shape[0], D), jnp.uint32),
        grid_spec=pltpu.PrefetchScalarGridSpec(
            num_scalar_prefetch=3,
            grid=(hs.shape[0] // TM_MOE,),
            in_specs=[
                pl.BlockSpec((TM_MOE, D), lambda j, elo, ehi, v: (j, 0)),
                lo_spec(up), lo_spec(up), lo_spec(down), hi_spec(up), hi_spec(up), hi_spec(down),
            ],
            out_specs=pl.BlockSpec((TM_MOE, D), lambda j, elo, ehi, v: (j, 0)),
        ),
        compiler_params=_params(("arbitrary",)),
        name="moe_ffn",
    )(tile_elo, tile_ehi, tile_valid, hs, w1, w3, w2, w1, w3, w2)


def _combine_kernel(starts_ref, idx_hbm, ys_hbm, x1_ref, route_ref, mod_ref, g_ref, b_ref, o_ref, *scratch):
    idx_s, sem_idx, buf, sem_rows = scratch[:4], scratch[4], scratch[5], scratch[6]
    _permute_rows(starts_ref, idx_hbm, idx_s, sem_idx,
                  lambda u, k, pos: pltpu.make_async_copy(ys_hbm.at[pl.ds(pos, 1)], buf.at[u, pl.ds(k, 1)],
                                                          sem_rows))

    packed = buf[...].reshape(T_ROW, D)
    y_lo = lax.bitcast_convert_type(lax.shift_left(packed, jnp.uint32(16)), F32)
    y_hi = lax.bitcast_convert_type(packed & jnp.uint32(0xFFFF0000), F32)
    moe = route_ref[:, 0:1] * y_lo + route_ref[:, 1:2] * y_hi
    o_ref[...] = _ln_rows(ALPHA * x1_ref[...] + mod_ref[5:6, :] * moe) * g_ref[...] + b_ref[...]


def _combine(starts, idx, ys, x1, route, mod, ln_g, ln_b, seq, n_lat, n_batch):
    n_rows = x1.shape[0]
    any_spec = pl.BlockSpec(memory_space=pl.ANY)
    mod_idx = _mod_index(T_ROW, seq, n_lat, n_batch)
    return pl.pallas_call(
        _combine_kernel,
        out_shape=jax.ShapeDtypeStruct((n_rows, D), F32),
        grid_spec=pltpu.PrefetchScalarGridSpec(
            num_scalar_prefetch=1,
            grid=(n_rows // T_ROW,),
            in_specs=[
                any_spec, any_spec,
                pl.BlockSpec((T_ROW, D), lambda i, st: (i, 0)),
                pl.BlockSpec((T_ROW, LANE), lambda i, st: (i, 0)),
                pl.BlockSpec((None, N_MOD, D), lambda i, st: mod_idx(i)),
                pl.BlockSpec((1, D), lambda i, st: (0, 0)), pl.BlockSpec((1, D), lambda i, st: (0, 0)),
            ],
            out_specs=pl.BlockSpec((T_ROW, D), lambda i, st: (i, 0)),
            scratch_shapes=_ROW_IDX_SCRATCH + [pltpu.VMEM((T_ROW // SUB, SUB, D), jnp.uint32),
                                               pltpu.SemaphoreType.DMA],
        ),
        compiler_params=_params(("arbitrary",)),
        name="combine",
    )(starts, idx, ys, x1, route, mod, ln_g, ln_b)


def _rot_matrix(width, start, half):
    r = np.zeros((width, width), np.float32)
    for j in range(half):
        r[start + half + j, start + j] = -1.0
        r[start + j, start + half + j] = 1.0
    return r


def _constants():
    ones_a = np.kron(np.eye(2, dtype=np.float32), np.ones((HD, HD), np.float32))
    rot_a = _rot_matrix(HB, 0, HD // 2) + _rot_matrix(HB, HD, HD // 2)
    rot_cq = _rot_matrix(HB, C_NOPE, C_ROPE // 2)
    rot_ck = _rot_matrix(HB, 0, C_ROPE // 2)
    place = np.zeros((HB, C_QW), np.float32)
    for h in range(C_HEADS):
        for j in range(C_ROPE):
            place[j, h * HB + C_NOPE + j] = 1.0
    tri = np.triu(np.ones((TM_OUT, TM_OUT), np.float32), 1)
    cls_lo, cls_hi = [], []
    for g in range(N_GROUPS):
        for lo in range(EPG):
            for hi in range(lo + 1, EPG):
                cls_lo.append(g * EPG + lo)
                cls_hi.append(g * EPG + hi)
    as_bf = lambda a: jnp.asarray(a, BF16)
    return dict(ones_a=as_bf(ones_a), rot_a=as_bf(rot_a), rot_cq=as_bf(rot_cq), rot_ck=as_bf(rot_ck),
                place_c=as_bf(place), tri=as_bf(tri), eye=as_bf(np.eye(HB, dtype=np.float32)),
                cls_lo=jnp.asarray(cls_lo, jnp.int32), cls_hi=jnp.asarray(cls_hi, jnp.int32))


def _rope_tables(seq):
    t = jnp.arange(seq, dtype=jnp.int32)
    row = (t // GRID_W).astype(F32)
    col = (t % GRID_W).astype(F32)

    def angles(dim):
        n_freq = dim // 4
        inv = THETA ** (-jnp.arange(n_freq, dtype=F32) / n_freq)
        ang = jnp.concatenate([row[:, None] * inv, col[:, None] * inv], -1)
        return jnp.concatenate([ang, jnp.zeros((TQ, dim // 2), F32)], 0)

    ang_a, ang_c = angles(HD), angles(C_ROPE)
    ones = lambda w: jnp.ones((seq + TQ, w), F32)
    zeros = lambda w: jnp.zeros((seq + TQ, w), F32)
    ca, sa = jnp.cos(ang_a), jnp.sin(ang_a)
    cc, sc = jnp.cos(ang_c), jnp.sin(ang_c)
    return dict(
        cos_a=jnp.tile(ca, (1, 4)), sin_a=jnp.tile(sa, (1, 4)),
        cos_cq=jnp.concatenate([ones(C_NOPE), cc, cc, ones(HB - C_NOPE - C_ROPE)], -1),
        sin_cq=jnp.concatenate([zeros(C_NOPE), sc, sc, zeros(HB - C_NOPE - C_ROPE)], -1),
        cos_ck=jnp.concatenate([cc, cc, ones(HB - C_ROPE)], -1),
        sin_ck=jnp.concatenate([sc, sc, zeros(HB - C_ROPE)], -1),
    )


def _pad_cols(w, width):
    return jnp.pad(w, ((0, 0), (0, width - w.shape[1])))


def _layer_weights(w_in, w_uq, w_ukv, w_out, w_rg, b_rg, w_re, b_re):
    pa = (A_HEADS + 2 * A_KV) * HD
    pb = 3 * B_HEADS * HD
    bw = B_HEADS * HD
    g = A_HEADS // A_KV
    qa = w_in[:, :A_HEADS * HD].reshape(D, A_KV, g, HD).transpose(0, 2, 1, 3).reshape(D, A_HEADS * HD)
    kva = w_in[:, A_HEADS * HD:pa]
    qb, kb, vb = (_pad_cols(w_in[:, pa + k * bw:pa + (k + 1) * bw], YW) for k in range(3))
    cq = w_in[:, pa + pb:pa + pb + C_QR]
    ckv = _pad_cols(w_in[:, pa + pb + C_QR:], 2 * C_KVR)
    w_all = jnp.concatenate([qa, kva, qb, kb, vb, cq, ckv], -1).astype(BF16)

    wuq = jnp.pad(w_uq.reshape(C_QR, C_HEADS, C_NOPE + C_ROPE), ((0, 0), (0, 0), (0, HB - C_NOPE - C_ROPE)))
    wuq = wuq.reshape(C_QR, C_QW).astype(BF16)
    ukv = w_ukv.reshape(C_KVR, C_HEADS, C_NOPE + C_V)
    wuk = jnp.pad(ukv[:, :, :C_NOPE], ((0, 0), (0, 0), (0, HB - C_NOPE))).reshape(C_KVR, C_QW).astype(BF16)
    wuvt = ukv[:, :, C_NOPE:].reshape(C_KVR, C_HEADS * C_V).T.astype(BF16)

    oa = w_out[:A_HEADS * HD].reshape(A_KV, g, HD, D).transpose(1, 0, 2, 3).reshape(A_HEADS * HD, D)
    ob = jnp.pad(w_out[A_HEADS * HD:A_HEADS * HD + bw], ((0, YW - bw), (0, 0)))
    oc = jnp.pad(w_out[A_HEADS * HD + bw:], ((C_V, 0), (0, 0)))
    w_parts = tuple(w.astype(BF16) for w in (oa, ob, oc))

    w_rt = jnp.zeros((ROUTER_ROWS, D), F32).at[:N_GROUPS].set(w_rg.T).at[EPG:].set(w_re.T).astype(BF16)
    b_rt = jnp.zeros((ROUTER_ROWS,), F32).at[:N_GROUPS].set(b_rg).at[EPG:].set(b_re)
    b_rt = jnp.broadcast_to(b_rt[:, None], (ROUTER_ROWS, LANE))
    return w_all, dict(wuq=wuq, wuk=wuk, wuvt=wuvt), w_parts, w_rt, b_rt


def kernel(x, c, ctx, c_ctx, w_mod, b_mod, w_in, q_gain_a, k_gain_a, rpb_b, q_lat_gain, kv_lat_gain,
           w_uq, w_ukv, w_out, ln1_g, ln1_b, w_rg, b_rg, w_re, b_re, w1, w3, w2, ln2_g, ln2_b):
    n_batch, seq, _ = x.shape
    ctx_len = ctx.shape[1]
    n_lat, n_ctx = n_batch * seq, n_batch * ctx_len
    assert ctx_len == TQ and seq % TM_IN == 0 and n_ctx % T_ROW == 0 and seq // GRID_W >= NA_BAND
    assert T_ROW == TM_OUT

    consts = _constants()
    tabs = _rope_tables(seq)
    mod_rows = -(-(n_batch + 1) // 8) * 8
    cc = jnp.zeros((mod_rows, D), F32).at[:n_batch].set(c).at[n_batch].set(c_ctx)
    xs = (x.reshape(n_lat, D), ctx.reshape(n_ctx, D))

    for l in range(DEPTH):
        with_ctx = l < DEPTH - 1
        mod = _modulation(cc, w_mod[l], b_mod[l]).reshape(mod_rows, N_MOD, D)
        w_all, mla_w, w_parts, w_rt, b_rt = _layer_weights(w_in[l], w_uq[l], w_ukv[l], w_out[l],
                                                          w_rg[l], b_rg[l], w_re[l], b_re[l])
        qa, kva, qb, kb, vb, cq, ckv = _inproj(xs, mod, w_all, seq, n_lat, n_batch)
        ya = _gqa(qa, kva, tabs, jnp.tile(q_gain_a[l], 2)[None], jnp.tile(k_gain_a[l], 2)[None],
                  consts, seq, ctx_len, n_batch, with_ctx)
        yb = _na(qb, kb, vb, _na_bias(rpb_b[l], seq // GRID_W, seq // TQ), consts["eye"],
                 seq, ctx_len, n_batch, with_ctx)
        yc = _mla(cq, ckv, tabs, q_lat_gain[l][None], kv_lat_gain[l][None], mla_w, consts,
                  seq, ctx_len, n_batch, with_ctx)

        n_rows = n_lat + n_ctx if with_ctx else n_lat
        x1, h2, route, idx, counts = _outproj(ya, yb, yc, xs, mod, w_parts, ln1_g[l][None], ln1_b[l][None],
                                              w_rt, b_rt, consts["tri"], n_rows, seq, n_lat, n_batch)

        cnt = counts[:N_CLS, 0].astype(jnp.int32)
        padded = (cnt + TM_MOE - 1) // TM_MOE * TM_MOE
        ends = jnp.cumsum(padded)
        starts = ends - padded
        n_tiles = n_rows // TM_MOE + N_CLS
        tile_start = jnp.arange(n_tiles, dtype=jnp.int32) * TM_MOE
        tile_cls = jnp.sum((ends[None, :] <= tile_start[:, None]).astype(jnp.int32), -1)
        tile_cls = jnp.minimum(tile_cls, N_CLS - 1)
        tile_valid = (tile_start < ends[-1]).astype(jnp.int32)

        hs = _dispatch(starts, idx, h2, jnp.zeros((n_tiles * TM_MOE, D), F32))
        ys = _moe(consts["cls_lo"][tile_cls], consts["cls_hi"][tile_cls], tile_valid, hs,
                  w1[l].astype(BF16), w3[l].astype(BF16), w2[l].astype(BF16))
        xs = (_combine(starts, idx, ys, x1, route, mod, ln2_g[l][None], ln2_b[l][None], seq, n_lat, n_batch),)

    return xs[0].reshape(n_batch, seq, D)
```

```python
import functools

import numpy as np
import jax
import jax.numpy as jnp
from jax import lax
from jax.experimental import pallas as pl
from jax.experimental.pallas import tpu as pltpu

D = 1024
GRID_W = 64
HD = 64
A_HEADS, A_KV = 6, 2
B_HEADS = 5
NA_R, NA_C = 8, 16
C_HEADS, C_QR, C_KVR, C_NOPE, C_ROPE, C_V = 5, 256, 128, 64, 32, 64
THETA = 10000.0
N_GROUPS, EPG, N_EXP, D_EXP = 4, 8, 32, 256
N_PAIR = EPG * (EPG - 1) // 2
N_CLS = N_GROUPS * N_PAIR
DEPTH = 2
ALPHA = (2 * DEPTH) ** 0.25
EPS = 1e-6
N_MOD = 6

LANE = 128
HB = 2 * HD
YW = 3 * HB

TM_IN = 512
TM_OUT = 512
TQ = 256
TM_MOE = 256
T_ROW = 512
VMEM_LIMIT = 56 * 1024 * 1024

F32 = jnp.float32
BF16 = jnp.bfloat16
NEG_INF = float("-inf")
LOG2E = 1.4426950408889634


def _dot(a, b):
    return jnp.dot(a, b, preferred_element_type=F32)


def _dot_nt(a, b):
    return lax.dot_general(a, b, (((1,), (1,)), ((), ())), preferred_element_type=F32)


def _params(sem, **kw):
    return pltpu.CompilerParams(dimension_semantics=sem, vmem_limit_bytes=VMEM_LIMIT, **kw)


def _ln_rows(x):
    mu = jnp.mean(x, -1, keepdims=True)
    xc = x - mu
    var = jnp.mean(xc * xc, -1, keepdims=True)
    return xc * lax.rsqrt(var + EPS)


def _silu(a):
    return a * (1.0 / (1.0 + jnp.exp(-a)))


DP = D // 2


def _pack_bf16_pairs(lo, hi):
    lo_bits = lax.bitcast_convert_type(lo.astype(BF16).astype(F32), jnp.uint32)
    hi_bits = lax.bitcast_convert_type(hi.astype(BF16).astype(F32), jnp.uint32)
    return lax.shift_right_logical(lo_bits, jnp.uint32(16)) | (hi_bits & jnp.uint32(0xFFFF0000))


def _unpack_bf16_pairs(w):
    lo = lax.bitcast_convert_type(lax.shift_left(w, jnp.uint32(16)), F32)
    hi = lax.bitcast_convert_type(w & jnp.uint32(0xFFFF0000), F32)
    return lo, hi


def _mod_kernel(c_ref, w_ref, b_ref, o_ref):
    s = _silu(c_ref[...])
    o_ref[...] = jnp.dot(s, w_ref[...], preferred_element_type=F32, precision=lax.Precision.HIGHEST) + b_ref[...]


def _modulation(cc, w_mod_l, b_mod_l):
    rows = cc.shape[0]
    tn = 1536
    return pl.pallas_call(
        _mod_kernel,
        out_shape=jax.ShapeDtypeStruct((rows, N_MOD * D), F32),
        grid=(N_MOD * D // tn,),
        in_specs=[
            pl.BlockSpec((rows, D), lambda j: (0, 0)),
            pl.BlockSpec((D, tn), lambda j: (0, j)),
            pl.BlockSpec((1, tn), lambda j: (0, j)),
        ],
        out_specs=pl.BlockSpec((rows, tn), lambda j: (0, j)),
        compiler_params=_params(("arbitrary",)),
        name="modulation",
    )(cc, w_mod_l, b_mod_l.reshape(1, -1))


IN_WIDTHS = (YW, 2 * HB, YW, YW, YW, C_QR, 2 * C_KVR)


def _for_row_source(x_refs, n_lat_tiles, body):
    if len(x_refs) == 1:
        body(x_refs[0])
        return
    i = pl.program_id(0)
    pl.when(i < n_lat_tiles)(lambda: body(x_refs[0]))
    pl.when(i >= n_lat_tiles)(lambda: body(x_refs[1]))


def _inproj_kernel(*refs, n_src, n_lat_tiles):
    x_refs, (mod_ref, w_ref), out_refs = refs[:n_src], refs[n_src:n_src + 2], refs[n_src + 2:]

    def body(x_ref):
        ln = _ln_rows(x_ref[...])
        h = (ln * (1.0 + mod_ref[1:2, :]) + mod_ref[0:1, :]).astype(BF16)
        p = _dot(h, w_ref[...])
        off = 0
        for o_ref, wd in zip(out_refs, IN_WIDTHS):
            o_ref[...] = p[:, off:off + wd].astype(BF16)
            off += wd

    _for_row_source(x_refs, n_lat_tiles, body)


def _mod_index(tile_rows, seq, n_lat, n_batch):
    tiles_per_batch = seq // tile_rows
    n_lat_tiles = n_lat // tile_rows
    return lambda i: (jnp.where(i < n_lat_tiles, i // tiles_per_batch, n_batch), 0, 0)


def _row_source_specs(xs, tile_rows, n_lat):
    if len(xs) == 1:
        return [pl.BlockSpec((tile_rows, D), lambda i: (i, 0))]
    nl = n_lat // tile_rows
    return [pl.BlockSpec((tile_rows, D), lambda i: (jnp.minimum(i, nl - 1), 0)),
            pl.BlockSpec((tile_rows, D), lambda i: (jnp.maximum(i - nl, 0), 0))]


def _inproj(xs, mod, w_all, seq, n_lat, n_batch):
    n = sum(x.shape[0] for x in xs)
    return pl.pallas_call(
        functools.partial(_inproj_kernel, n_src=len(xs), n_lat_tiles=n_lat // TM_IN),
        out_shape=[jax.ShapeDtypeStruct((n, wd), BF16) for wd in IN_WIDTHS],
        grid=(n // TM_IN,),
        in_specs=_row_source_specs(xs, TM_IN, n_lat) + [
            pl.BlockSpec((None, N_MOD, D), _mod_index(TM_IN, seq, n_lat, n_batch)),
            pl.BlockSpec((D, sum(IN_WIDTHS)), lambda i: (0, 0)),
        ],
        out_specs=[pl.BlockSpec((TM_IN, wd), lambda i: (i, 0)) for wd in IN_WIDTHS],
        compiler_params=_params(("arbitrary",)),
        name="inproj",
    )(*xs, mod, w_all)


KEY_CHUNK = 64


def _scores(task):
    qm, key_thunks, _, bias_thunks = task
    chunks = [k() for k in key_thunks]
    s = _dot_nt(jnp.concatenate(chunks, 0), qm())
    parts, r0 = [], 0
    for c, k in enumerate(chunks):
        part = s[r0:r0 + k.shape[0]]
        if bias_thunks is not None and bias_thunks[c] is not None:
            part = part + bias_thunks[c]()
        parts.append(part)
        r0 += k.shape[0]
    return parts


def _softmax(parts):
    blocks = [s[c:c + KEY_CHUNK] for s in parts for c in range(0, s.shape[0], KEY_CHUNK)]
    m = jnp.max(functools.reduce(jnp.maximum, blocks), 0, keepdims=True)
    m_b = jnp.broadcast_to(m, blocks[0].shape)
    acc, p_blocks = None, []
    for blk in blocks:
        p = jnp.exp2(blk - m_b)
        acc = p if acc is None else acc + p
        p_blocks.append(p.astype(BF16))
    return jnp.concatenate(p_blocks, 0), jnp.sum(acc, 0, keepdims=True)


def _attend_heads(tasks):
    outs, scored, soft = [], None, None
    for task in list(tasks) + [None, None]:
        nxt = None if task is None else (_scores(task), task[2])
        nxt_soft = None if scored is None else _softmax(scored[0]) + (scored[1],)
        if soft is not None:
            p, den, vt_thunks = soft
            outs.append(_dot(jnp.concatenate([v() for v in vt_thunks], 1), p) * (1.0 / den))
        scored, soft = nxt, nxt_soft
    return outs


def _row_chunks(ref, n_rows, lanes=slice(None)):
    return [lambda c=c: ref[c * TQ:(c + 1) * TQ, lanes] for c in range(n_rows // TQ)]


def _col_chunks(ref, n_cols, rows):
    return [lambda c=c: ref[rows, c * TQ:(c + 1) * TQ] for c in range(n_cols // TQ)]


def _half_query(q, half):
    first = lax.broadcasted_iota(jnp.int32, q.shape, 1) < HD
    return lambda: jnp.where(first if half == 0 else ~first, q, 0.0).astype(BF16)


def _pair_query(q):
    return lambda: jnp.concatenate([_half_query(q, 0)(), _half_query(q, 1)()], 0)


def _head_pair_rows(lo_t, hi_t):
    return jnp.concatenate([lo_t, hi_t], 0).T


def _pair_rows(o):
    tq = o.shape[1] // 2
    return _head_pair_rows(o[0:HD, 0:tq], o[HD:2 * HD, tq:2 * tq])


def _transpose_rows(eye_ref, x):
    return _dot_nt(eye_ref[...], x).astype(BF16)


def _rope_rows(x, cos, sin, rot_ref):
    return x * cos + _dot(x.astype(BF16), rot_ref[...]) * sin


def _attn_specs(n_batch, seq, ctx_len, q_width, kv_width):
    nq = seq // TQ
    ctx0 = n_batch * seq // ctx_len

    def q_idx(b, i):
        return (jnp.where(i < nq, b * nq + i, ctx0 + b), 0)

    return dict(
        q=lambda w: pl.BlockSpec((TQ, w), q_idx),
        lat=lambda w: pl.BlockSpec((seq, w), lambda b, i: (b, 0)),
        ctx=lambda w: pl.BlockSpec((ctx_len, w), lambda b, i: (ctx0 + b, 0)),
        tab=pl.BlockSpec((TQ, HB), lambda b, i: (i, 0)),
        full=lambda shape: pl.BlockSpec(shape, lambda b, i: (0,) * len(shape)),
    )


def _gqa_kernel(q_ref, kvl_ref, kvc_ref, cosq_ref, sinq_ref, cosk_ref, sink_ref, qg_ref, kg_ref,
                ones_ref, rot_ref, eye_ref, o_ref, kl_s, kc_s, vtl_s, vtc_s, *, nq):
    i = pl.program_id(1)

    def head_rms(x):
        ss = _dot((x * x).astype(BF16), ones_ref[...])
        return lax.rsqrt(ss * (1.0 / HD) + EPS)

    @pl.when(i == 0)
    def _():
        kl = kvl_ref[:, 0:HB].astype(F32)
        kn = kl * head_rms(kl) * kg_ref[...]
        kl_s[...] = _rope_rows(kn, cosk_ref[...], sink_ref[...], rot_ref).astype(BF16)
        kc = kvc_ref[:, 0:HB].astype(F32)
        kc_s[...] = (kc * head_rms(kc) * kg_ref[...]).astype(BF16)
        vtl_s[...] = _transpose_rows(eye_ref, kvl_ref[:, HB:2 * HB])
        vtc_s[...] = _transpose_rows(eye_ref, kvc_ref[:, HB:2 * HB])

    seq, ctx_len = kl_s.shape[0], kc_s.shape[0]

    def run(with_latent):
        keys = (_row_chunks(kl_s, seq) if with_latent else []) + _row_chunks(kc_s, ctx_len)
        tasks = []
        for blk in range(A_HEADS // 2):
            q = q_ref[:, blk * HB:(blk + 1) * HB].astype(F32)
            qn = q * head_rms(q) * qg_ref[...]
            qr = _rope_rows(qn, cosq_ref[...], sinq_ref[...], rot_ref) * (HD ** -0.5 * LOG2E)
            vts = (_col_chunks(vtl_s, seq, slice(None)) if with_latent else []) \
                + _col_chunks(vtc_s, ctx_len, slice(None))
            tasks.append((_pair_query(qr), keys, vts, None))
        outs = _attend_heads(tasks)
        for blk in range(A_HEADS // 2):
            o_ref[:, blk * HB:(blk + 1) * HB] = _pair_rows(outs[blk]).astype(BF16)

    @pl.when(i < nq)
    def _():
        run(True)

    @pl.when(i == nq)
    def _():
        run(False)


def _gqa(qa, kva, tabs, q_gain, k_gain, consts, seq, ctx_len, n_batch, with_ctx):
    nq = seq // TQ
    sp = _attn_specs(n_batch, seq, ctx_len, YW, 2 * HB)
    n_out = n_batch * (seq + ctx_len) if with_ctx else n_batch * seq
    return pl.pallas_call(
        functools.partial(_gqa_kernel, nq=nq),
        out_shape=jax.ShapeDtypeStruct((n_out, YW), BF16),
        grid=(n_batch, nq + int(with_ctx)),
        in_specs=[
            sp["q"](YW), sp["lat"](2 * HB), sp["ctx"](2 * HB), sp["tab"], sp["tab"],
            sp["full"]((seq, HB)), sp["full"]((seq, HB)),
            sp["full"]((1, HB)), sp["full"]((1, HB)),
            sp["full"]((HB, HB)), sp["full"]((HB, HB)), sp["full"]((HB, HB)),
        ],
        out_specs=sp["q"](YW),
        scratch_shapes=[pltpu.VMEM((seq, HB), BF16), pltpu.VMEM((ctx_len, HB), BF16),
                        pltpu.VMEM((HB, seq), BF16), pltpu.VMEM((HB, ctx_len), BF16)],
        compiler_params=_params(("arbitrary", "arbitrary")),
        name="gqa",
    )(qa, kva, kva, tabs["cos_a"], tabs["sin_a"], tabs["cos_a"], tabs["sin_a"], q_gain, k_gain,
      consts["ones_a"], consts["rot_a"], consts["eye"])


NA_TILE_ROWS = TQ // GRID_W
NA_BAND = NA_R + NA_TILE_ROWS
NA_KEYS = NA_BAND * GRID_W


def _na_band_start(i, rows):
    return jnp.clip(i * NA_TILE_ROWS - NA_R // 2, 0, rows - NA_BAND)


def _na_kernel(q_ref, kl_ref, vl_ref, kc_ref, vc_ref, bias_ref, eye_ref, o_ref, vtl_s, vtc_s, *, rows, nq):
    i = pl.program_id(1)
    scale = HD ** -0.5 * LOG2E
    band_tiles = NA_KEYS // TQ

    @pl.when(i == 0)
    def _():
        for blk in range(3):
            lanes = slice(blk * HB, (blk + 1) * HB)
            for t in range(nq):
                vtl_s[t, lanes, :] = _transpose_rows(eye_ref, vl_ref[t * TQ:(t + 1) * TQ, lanes])
            vtc_s[lanes, :] = _transpose_rows(eye_ref, vc_ref[:, lanes])

    def run(t0):
        tasks = []
        for blk in range(3):
            paired = 2 * blk + 1 < B_HEADS
            lanes = slice(blk * HB, (blk + 1) * HB)
            hrows = lanes if paired else slice(blk * HB, blk * HB + HD)
            q = q_ref[:, lanes].astype(F32) * scale
            keys = [lambda lanes=lanes: kc_ref[:, lanes]]
            vts = [lambda hrows=hrows: vtc_s[hrows, :]]
            biases = None
            if t0 is not None:
                keys = [lambda j=j, lanes=lanes: kl_ref[pl.ds(pl.multiple_of((t0 + j) * TQ, TQ), TQ), lanes]
                        for j in range(band_tiles)] + keys
                vts = [lambda j=j, hrows=hrows: vtl_s[t0 + j, hrows, :] for j in range(band_tiles)] + vts

                def bias_chunk(j, blk=blk, paired=paired):
                    rows = slice(j * TQ, (j + 1) * TQ)
                    if not paired:
                        return bias_ref[2 * blk, rows, :]
                    return jnp.concatenate([bias_ref[2 * blk, rows, :], bias_ref[2 * blk + 1, rows, :]], 1)

                biases = [functools.partial(bias_chunk, j) for j in range(band_tiles)] + [None]
            tasks.append((_pair_query(q) if paired else _half_query(q, 0), keys, vts, biases))
        outs = _attend_heads(tasks)
        for blk in range(3):
            o = outs[blk]
            y = _pair_rows(o) if o.shape[0] == HB else _head_pair_rows(o, jnp.zeros_like(o))
            o_ref[:, blk * HB:(blk + 1) * HB] = y.astype(BF16)

    @pl.when(i < nq)
    def _():
        run(_na_band_start(i, rows) // NA_TILE_ROWS)

    @pl.when(i == nq)
    def _():
        run(None)


def _na_tile_config(i, nq):
    return jnp.where(i == 0, 0, jnp.where(i >= nq - 1, 2, 1))


def _na(qb, kb, vb, bias, eye, seq, ctx_len, n_batch, with_ctx):
    nq = seq // TQ
    sp = _attn_specs(n_batch, seq, ctx_len, YW, YW)
    n_out = n_batch * (seq + ctx_len) if with_ctx else n_batch * seq
    bias_spec = pl.BlockSpec((None,) + bias.shape[1:], lambda b, i: (_na_tile_config(i, nq), 0, 0, 0))
    return pl.pallas_call(
        functools.partial(_na_kernel, rows=seq // GRID_W, nq=nq),
        out_shape=jax.ShapeDtypeStruct((n_out, YW), BF16),
        grid=(n_batch, nq + int(with_ctx)),
        in_specs=[sp["q"](YW), sp["lat"](YW), sp["lat"](YW), sp["ctx"](YW), sp["ctx"](YW), bias_spec,
                  sp["full"]((HB, HB))],
        out_specs=sp["q"](YW),
        scratch_shapes=[pltpu.VMEM((nq, YW, TQ), BF16), pltpu.VMEM((YW, ctx_len), BF16)],
        compiler_params=_params(("arbitrary", "arbitrary")),
        name="na",
    )(qb, kb, vb, kb, vb, bias, eye)


def _na_window_pattern(i, rows):
    r = i * NA_TILE_ROWS + np.arange(NA_TILE_ROWS)
    r0 = np.clip(r - NA_R // 2, 0, rows - NA_R)
    krow = int(np.clip(i * NA_TILE_ROWS - NA_R // 2, 0, rows - NA_BAND)) + np.arange(NA_BAND)
    valid = (krow[None, :] >= r0[:, None]) & (krow[None, :] < r0[:, None] + NA_R)
    dr = np.clip(krow[None, :] - r[:, None] + (NA_R - 1), 0, 2 * NA_R - 2)
    return valid, dr


def _na_bias(rpb, rows, nq):
    col = jnp.arange(GRID_W, dtype=jnp.int32)
    c0 = jnp.clip(col - NA_C // 2, 0, GRID_W - NA_C)
    col_in = (col[None, :] >= c0[:, None]) & (col[None, :] < c0[:, None] + NA_C)
    dc = jnp.clip(col[None, :] - col[:, None] + (NA_C - 1), 0, 2 * NA_C - 2)
    tbl = jnp.where(col_in[None, None], rpb[:, :, dc] * LOG2E, NEG_INF)
    patterns = [_na_window_pattern(i, rows) for i in range(nq)]
    for i in range(2, nq - 1):
        assert all(np.array_equal(a, b) for a, b in zip(patterns[i], patterns[1]))
    out = []
    for valid, dr in (patterns[0], patterns[1], patterns[nq - 1]):
        band = jnp.where(jnp.asarray(valid)[None, :, :, None, None], tbl[:, jnp.asarray(dr)], NEG_INF)
        out.append(band.transpose(0, 2, 4, 1, 3).reshape(B_HEADS, NA_KEYS, TQ))
    return jnp.stack(out).astype(F32)


C_QW = C_HEADS * HB


def _mla_kernel(cq_ref, kvl_ref, kvc_ref, cosq_ref, sinq_ref, cosk_ref, sink_ref, qg_ref, kg_ref,
                wuq_ref, wuk_ref, wuvt_ref, place_ref, rotq_ref, rotk_ref, o_ref, kl_s, kc_s, vl_s, vc_s, *, nq):
    i = pl.program_id(1)

    def latent_rms(x, gain):
        ms = jnp.mean(x * x, -1, keepdims=True)
        return (x * lax.rsqrt(ms + EPS) * gain).astype(BF16)

    @pl.when(i == 0)
    def _():
        cl = latent_rms(kvl_ref[:, 0:C_KVR].astype(F32), kg_ref[...])
        rl = _rope_rows(kvl_ref[:, C_KVR:2 * C_KVR].astype(F32), cosk_ref[...], sink_ref[...], rotk_ref)
        kl_s[...] = (_dot(cl, wuk_ref[...]) + _dot(rl.astype(BF16), place_ref[...])).astype(BF16)
        vl_s[...] = _dot_nt(wuvt_ref[...], cl).astype(BF16)
        cc = latent_rms(kvc_ref[:, 0:C_KVR].astype(F32), kg_ref[...])
        kc_s[...] = (_dot(cc, wuk_ref[...]) + _dot(kvc_ref[:, C_KVR:2 * C_KVR], place_ref[...])).astype(BF16)
        vc_s[...] = _dot_nt(wuvt_ref[...], cc).astype(BF16)

    seq, ctx_len = kl_s.shape[0], kc_s.shape[0]

    def run(with_latent):
        cq = latent_rms(cq_ref[...].astype(F32), qg_ref[...])
        tasks = []
        for h in range(C_HEADS):
            lanes = slice(h * HB, (h + 1) * HB)
            q = _dot(cq, wuq_ref[:, lanes])
            qr = _rope_rows(q, cosq_ref[...], sinq_ref[...], rotq_ref) * ((C_NOPE + C_ROPE) ** -0.5 * LOG2E)
            hrows = slice(h * C_V, (h + 1) * C_V)
            keys, vts = _row_chunks(kc_s, ctx_len, lanes), _col_chunks(vc_s, ctx_len, hrows)
            if with_latent:
                keys, vts = _row_chunks(kl_s, seq, lanes) + keys, _col_chunks(vl_s, seq, hrows) + vts
            tasks.append((lambda qr=qr: qr.astype(BF16), keys, vts, None))
        outs = _attend_heads(tasks)
        outs = [jnp.zeros_like(outs[0])] + outs
        for blk in range(3):
            o_ref[:, blk * HB:(blk + 1) * HB] = _head_pair_rows(outs[2 * blk], outs[2 * blk + 1]).astype(BF16)

    @pl.when(i < nq)
    def _():
        run(True)

    @pl.when(i == nq)
    def _():
        run(False)


def _mla(cq, ckv, tabs, q_gain, kv_gain, wts, consts, seq, ctx_len, n_batch, with_ctx):
    nq = seq // TQ
    sp = _attn_specs(n_batch, seq, ctx_len, C_QR, 2 * C_KVR)
    full = sp["full"]
    n_out = n_batch * (seq + ctx_len) if with_ctx else n_batch * seq
    return pl.pallas_call(
        functools.partial(_mla_kernel, nq=nq),
        out_shape=jax.ShapeDtypeStruct((n_out, YW), BF16),
        grid=(n_batch, nq + int(with_ctx)),
        in_specs=[
            sp["q"](C_QR), sp["lat"](2 * C_KVR), sp["ctx"](2 * C_KVR), sp["tab"], sp["tab"],
            full((seq, HB)), full((seq, HB)), full((1, C_QR)), full((1, C_KVR)),
            full((C_QR, C_QW)), full((C_KVR, C_QW)), full((C_HEADS * C_V, C_KVR)),
            full((HB, C_QW)), full((HB, HB)), full((HB, HB)),
        ],
        out_specs=sp["q"](YW),
        scratch_shapes=[pltpu.VMEM((seq, C_QW), BF16), pltpu.VMEM((ctx_len, C_QW), BF16),
                        pltpu.VMEM((C_HEADS * C_V, seq), BF16), pltpu.VMEM((C_HEADS * C_V, ctx_len), BF16)],
        compiler_params=_params(("arbitrary", "arbitrary")),
        name="mla",
    )(cq, ckv, ckv, tabs["cos_cq"], tabs["sin_cq"], tabs["cos_ck"], tabs["sin_ck"], q_gain, kv_gain,
      wts["wuq"], wts["wuk"], wts["wuvt"], consts["place_c"], consts["rot_cq"], consts["rot_ck"])


ROUTER_ROWS = (N_GROUPS + 1) * EPG


def _outproj_kernel(*refs, n_src, n_lat_tiles):
    x_refs = refs[:n_src]
    (ya_ref, yb_ref, yc_ref, mod_ref, wa_ref, wb_ref, wc_ref, g_ref, b_ref, wrt_ref, brt_ref, tri_ref,
     x1_ref, h2_ref, route_ref, idx_ref, cnt_ref, carry_s) = refs[n_src:]
    i = pl.program_id(0)

    @pl.when(i == 0)
    def _():
        carry_s[...] = jnp.zeros_like(carry_s)

    y = _dot(ya_ref[...], wa_ref[...]) + _dot(yb_ref[...], wb_ref[...]) + _dot(yc_ref[...], wc_ref[...])

    def residual(x_ref):
        x1_ref[...] = _ln_rows(ALPHA * x_ref[...] + mod_ref[2:3, :] * y) * g_ref[...] + b_ref[...]

    _for_row_source(x_refs, n_lat_tiles, residual)
    h2 = _ln_rows(x1_ref[...]) * (1.0 + mod_ref[4:5, :]) + mod_ref[3:4, :]
    h2_ref[...] = _pack_bf16_pairs(h2[:, 0:DP], h2[:, DP:D])

    logit = _dot_nt(wrt_ref[...], h2.astype(BF16)) + brt_ref[:, 0:1]
    sub = lax.broadcasted_iota(jnp.int32, (EPG, TM_OUT), 0)

    def col_max(v):
        return jnp.max(v, 0, keepdims=True)

    def first_row(mask):
        return jnp.min(jnp.where(mask, sub, EPG), 0, keepdims=True)

    g_logit = jnp.where(sub < N_GROUPS, logit[0:EPG], NEG_INF)
    g_max = col_max(g_logit)
    g_idx = first_row(g_logit == g_max)
    g_w = 1.0 / jnp.sum(jnp.exp(g_logit - g_max), 0, keepdims=True)
    e_logit = logit[EPG:2 * EPG]
    for g in range(1, N_GROUPS):
        e_logit = jnp.where(g_idx == g, logit[(g + 1) * EPG:(g + 2) * EPG], e_logit)
    m1 = col_max(e_logit)
    i1 = first_row(e_logit == m1)
    rest = jnp.where(sub == i1, NEG_INF, e_logit)
    m2 = col_max(rest)
    i2 = first_row(rest == m2)
    t = jnp.exp(m2 - m1)
    w1 = g_w / (1.0 + t)
    w2 = g_w * t / (1.0 + t)
    lo, hi = jnp.minimum(i1, i2), jnp.maximum(i1, i2)
    pair = lax.shift_right_logical(lo * (2 * EPG - 1 - lo), jnp.ones_like(lo)) + hi - lo - 1
    cls = g_idx * N_PAIR + pair
    w_lo = jnp.where(i1 < i2, w1, w2)
    w_hi = jnp.where(i1 < i2, w2, w1)

    cls_row = lax.broadcasted_iota(jnp.int32, (LANE, TM_OUT), 0)
    onehot = (cls_row == cls).astype(F32)
    prefix = _dot(onehot.astype(BF16), tri_ref[...]) + carry_s[:, 0:1]
    rank = jnp.sum(onehot * prefix, 0, keepdims=True)
    carry_s[...] = carry_s[...] + jnp.sum(onehot, 1, keepdims=True)
    cnt_ref[...] = carry_s[...]

    idx_ref[...] = jnp.where(sub == 0, cls, jnp.where(sub == 1, rank.astype(jnp.int32), 0))
    rows = jnp.where(sub == 0, w_lo, jnp.where(sub == 1, w_hi, 0.0))
    route_ref[...] = jnp.concatenate([rows, jnp.zeros((LANE - EPG, TM_OUT), F32)], 0).T


def _outproj(ya, yb, yc, xs, mod, w_parts, ln_g, ln_b, w_rt, b_rt, tri, n_rows, seq, n_lat, n_batch):
    row = lambda w: pl.BlockSpec((TM_OUT, w), lambda i: (i, 0))
    full = lambda shape: pl.BlockSpec(shape, lambda i: (0, 0))
    n_tiles = n_rows // TM_OUT
    return pl.pallas_call(
        functools.partial(_outproj_kernel, n_src=len(xs), n_lat_tiles=n_lat // TM_OUT),
        out_shape=[
            jax.ShapeDtypeStruct((n_rows, D), F32),
            jax.ShapeDtypeStruct((n_rows, DP), jnp.uint32),
            jax.ShapeDtypeStruct((n_rows, LANE), F32),
            jax.ShapeDtypeStruct((n_tiles, EPG, TM_OUT), jnp.int32),
            jax.ShapeDtypeStruct((LANE, LANE), F32),
        ],
        grid=(n_tiles,),
        in_specs=_row_source_specs(xs, TM_OUT, n_lat) + [
            row(YW), row(YW), row(YW),
            pl.BlockSpec((None, N_MOD, D), _mod_index(TM_OUT, seq, n_lat, n_batch)),
            full((YW, D)), full((YW, D)), full((YW, D)),
            full((1, D)), full((1, D)), full((ROUTER_ROWS, D)), full((ROUTER_ROWS, LANE)), full((TM_OUT, TM_OUT)),
        ],
        out_specs=[row(D), row(DP), row(LANE), pl.BlockSpec((None, EPG, TM_OUT), lambda i: (i, 0, 0)),
                   full((LANE, LANE))],
        scratch_shapes=[pltpu.VMEM((LANE, LANE), F32)],
        compiler_params=_params(("arbitrary",)),
        name="outproj",
    )(*xs, ya, yb, yc, mod, *w_parts, ln_g, ln_b, w_rt, b_rt, tri)


SUB = 8


def _wait_rows(make_copy):
    def wait(u, c):
        for k in range(SUB):
            make_copy(0, 0, 0).wait()
        return c

    lax.fori_loop(0, T_ROW // SUB, wait, 0)


def _permute_rows(starts_ref, idx_hbm, idx_s, sem_idx, make_copy, n=None, drain=True):
    i = pl.program_id(0)
    n = pl.num_programs(0) if n is None else n

    def index_copies(step, slot):
        return [pltpu.make_async_copy(idx_hbm.at[step, k], idx_s[2 * slot + k], sem_idx.at[slot, k])
                for k in range(2)]

    @pl.when(i == 0)
    def _():
        for cp in index_copies(0, 0):
            cp.start()

    for slot in range(2):

        @pl.when(lax.rem(i, 2) == slot)
        def _(slot=slot):
            for cp in index_copies(i, slot):
                cp.wait()

            @pl.when(i + 1 < n)
            def _():
                for cp in index_copies(i + 1, 1 - slot):
                    cp.start()

            cls_s, rank_s = idx_s[2 * slot], idx_s[2 * slot + 1]
            copy = make_copy(slot)

            def start(u, c):
                for k in range(SUB):
                    t = u * SUB + k
                    copy(u, k, starts_ref[cls_s[t]] + rank_s[t]).start(priority=k % 2)
                return c

            lax.fori_loop(0, T_ROW // SUB, start, 0)
            if drain:
                _wait_rows(copy)


def _dispatch_kernel(starts_ref, idx_hbm, h_ref, hs_in, hs_out, *scratch):
    del hs_in
    idx_s, sem_idx, sem_rows = scratch[:4], scratch[4], scratch[5]
    _permute_rows(starts_ref, idx_hbm, idx_s, sem_idx,
                  lambda slot: lambda u, k, pos: pltpu.make_async_copy(
                      h_ref.at[u, pl.ds(k, 1)], hs_out.at[pl.ds(pos, 1)], sem_rows))


_ROW_IDX_SCRATCH = [pltpu.SMEM((T_ROW,), jnp.int32)] * 4 + [pltpu.SemaphoreType.DMA((2, 2))]


def _dispatch(starts, idx, h2, hs0):
    any_spec = pl.BlockSpec(memory_space=pl.ANY)
    return pl.pallas_call(
        _dispatch_kernel,
        out_shape=jax.ShapeDtypeStruct(hs0.shape, hs0.dtype),
        grid_spec=pltpu.PrefetchScalarGridSpec(
            num_scalar_prefetch=1,
            grid=(idx.shape[0],),
            in_specs=[any_spec, pl.BlockSpec((T_ROW // SUB, SUB, DP), lambda i, st: (i, 0, 0)), any_spec],
            out_specs=any_spec,
            scratch_shapes=_ROW_IDX_SCRATCH + [pltpu.SemaphoreType.DMA],
        ),
        input_output_aliases={3: 0},
        compiler_params=_params(("arbitrary",)),
        name="dispatch",
    )(starts, idx, h2.reshape(-1, SUB, DP), hs0)


def _moe_kernel(elo_ref, ehi_ref, valid_ref, h_ref, w1a, w3a, w2a, w1b, w3b, w2b, o_ref):
    j = pl.program_id(0)

    def ffn(h, w1, w3, w2):
        hid = (_silu(_dot(h, w1[...])) * _dot(h, w3[...])).astype(BF16)
        return _dot(hid, w2[...])

    @pl.when(valid_ref[j] != 0)
    def _():
        h = jnp.concatenate(_unpack_bf16_pairs(h_ref[...]), 1).astype(BF16)
        o_ref[...] = _pack_bf16_pairs(ffn(h, w1a, w3a, w2a), ffn(h, w1b, w3b, w2b))

    @pl.when(valid_ref[j] == 0)
    def _():
        o_ref[...] = jnp.zeros_like(o_ref)


def _moe(tile_elo, tile_ehi, tile_valid, hs, w1, w3, w2):
    lo_spec = lambda shape: pl.BlockSpec((None,) + shape, lambda j, elo, ehi, v: (elo[j], 0, 0))
    hi_spec = lambda shape: pl.BlockSpec((None,) + shape, lambda j, elo, ehi, v: (ehi[j], 0, 0))
    up, down = (D, D_EXP), (D_EXP, D)
    return pl.pallas_call(
        _moe_kernel,
        out_shape=jax.ShapeDtypeStruct((hs.shape[0], D), jnp.uint32),
        grid_spec=pltpu.PrefetchScalarGridSpec(
            num_scalar_prefetch=3,
            grid=(hs.shape[0] // TM_MOE,),
            in_specs=[
                pl.BlockSpec((TM_MOE, DP), lambda j, elo, ehi, v: (j, 0)),
                lo_spec(up), lo_spec(up), lo_spec(down), hi_spec(up), hi_spec(up), hi_spec(down),
            ],
            out_specs=pl.BlockSpec((TM_MOE, D), lambda j, elo, ehi, v: (j, 0)),
        ),
        compiler_params=_params(("arbitrary",)),
        name="moe_ffn",
    )(tile_elo, tile_ehi, tile_valid, hs, w1, w3, w2, w1, w3, w2)


def _combine_kernel(starts_ref, idx_hbm, ys_hbm, x1_ref, route_ref, mod_ref, g_ref, b_ref, o_ref, *scratch):
    idx_s, sem_idx, buf, sem_rows = scratch[:4], scratch[4], scratch[5], scratch[6]
    i = pl.program_id(0)
    n_tiles = pl.num_programs(0) - 1

    def gather(slot):
        return lambda u, k, pos: pltpu.make_async_copy(ys_hbm.at[pl.ds(pos, 1)], buf.at[slot, u, pl.ds(k, 1)],
                                                       sem_rows.at[slot])

    @pl.when(i < n_tiles)
    def _():
        _permute_rows(starts_ref, idx_hbm, idx_s, sem_idx, gather, n=n_tiles, drain=False)

    for slot in range(2):

        @pl.when((i > 0) & (lax.rem(i + 1, 2) == slot))
        def _(slot=slot):
            _wait_rows(gather(slot))
            y_lo, y_hi = _unpack_bf16_pairs(buf[slot].reshape(T_ROW, D))
            moe = route_ref[:, 0:1] * y_lo + route_ref[:, 1:2] * y_hi
            o_ref[...] = _ln_rows(ALPHA * x1_ref[...] + mod_ref[5:6, :] * moe) * g_ref[...] + b_ref[...]


def _combine(starts, idx, ys, x1, route, mod, ln_g, ln_b, seq, n_lat, n_batch):
    n_rows = x1.shape[0]
    any_spec = pl.BlockSpec(memory_space=pl.ANY)
    mod_idx = _mod_index(T_ROW, seq, n_lat, n_batch)
    tile = lambda i: jnp.maximum(i - 1, 0)
    return pl.pallas_call(
        _combine_kernel,
        out_shape=jax.ShapeDtypeStruct((n_rows, D), F32),
        grid_spec=pltpu.PrefetchScalarGridSpec(
            num_scalar_prefetch=1,
            grid=(n_rows // T_ROW + 1,),
            in_specs=[
                any_spec, any_spec,
                pl.BlockSpec((T_ROW, D), lambda i, st: (tile(i), 0)),
                pl.BlockSpec((T_ROW, LANE), lambda i, st: (tile(i), 0)),
                pl.BlockSpec((None, N_MOD, D), lambda i, st: mod_idx(tile(i))),
                pl.BlockSpec((1, D), lambda i, st: (0, 0)), pl.BlockSpec((1, D), lambda i, st: (0, 0)),
            ],
            out_specs=pl.BlockSpec((T_ROW, D), lambda i, st: (tile(i), 0)),
            scratch_shapes=_ROW_IDX_SCRATCH + [pltpu.VMEM((2, T_ROW // SUB, SUB, D), jnp.uint32),
                                               pltpu.SemaphoreType.DMA((2,))],
        ),
        compiler_params=_params(("arbitrary",)),
        name="combine",
    )(starts, idx, ys, x1, route, mod, ln_g, ln_b)


def _rot_matrix(width, start, half):
    r = np.zeros((width, width), np.float32)
    for j in range(half):
        r[start + half + j, start + j] = -1.0
        r[start + j, start + half + j] = 1.0
    return r


def _constants():
    ones_a = np.kron(np.eye(2, dtype=np.float32), np.ones((HD, HD), np.float32))
    rot_a = _rot_matrix(HB, 0, HD // 2) + _rot_matrix(HB, HD, HD // 2)
    rot_cq = _rot_matrix(HB, C_NOPE, C_ROPE // 2)
    rot_ck = _rot_matrix(HB, 0, C_ROPE // 2)
    place = np.zeros((HB, C_QW), np.float32)
    for h in range(C_HEADS):
        for j in range(C_ROPE):
            place[j, h * HB + C_NOPE + j] = 1.0
    tri = np.triu(np.ones((TM_OUT, TM_OUT), np.float32), 1)
    cls_lo, cls_hi = [], []
    for g in range(N_GROUPS):
        for lo in range(EPG):
            for hi in range(lo + 1, EPG):
                cls_lo.append(g * EPG + lo)
                cls_hi.append(g * EPG + hi)
    as_bf = lambda a: jnp.asarray(a, BF16)
    return dict(ones_a=as_bf(ones_a), rot_a=as_bf(rot_a), rot_cq=as_bf(rot_cq), rot_ck=as_bf(rot_ck),
                place_c=as_bf(place), tri=as_bf(tri), eye=as_bf(np.eye(HB, dtype=np.float32)),
                cls_lo=jnp.asarray(cls_lo, jnp.int32), cls_hi=jnp.asarray(cls_hi, jnp.int32))


def _rope_tables(seq):
    t = jnp.arange(seq, dtype=jnp.int32)
    row = (t // GRID_W).astype(F32)
    col = (t % GRID_W).astype(F32)

    def angles(dim):
        n_freq = dim // 4
        inv = THETA ** (-jnp.arange(n_freq, dtype=F32) / n_freq)
        ang = jnp.concatenate([row[:, None] * inv, col[:, None] * inv], -1)
        return jnp.concatenate([ang, jnp.zeros((TQ, dim // 2), F32)], 0)

    ang_a, ang_c = angles(HD), angles(C_ROPE)
    ones = lambda w: jnp.ones((seq + TQ, w), F32)
    zeros = lambda w: jnp.zeros((seq + TQ, w), F32)
    ca, sa = jnp.cos(ang_a), jnp.sin(ang_a)
    cc, sc = jnp.cos(ang_c), jnp.sin(ang_c)
    return dict(
        cos_a=jnp.tile(ca, (1, 4)), sin_a=jnp.tile(sa, (1, 4)),
        cos_cq=jnp.concatenate([ones(C_NOPE), cc, cc, ones(HB - C_NOPE - C_ROPE)], -1),
        sin_cq=jnp.concatenate([zeros(C_NOPE), sc, sc, zeros(HB - C_NOPE - C_ROPE)], -1),
        cos_ck=jnp.concatenate([cc, cc, ones(HB - C_ROPE)], -1),
        sin_ck=jnp.concatenate([sc, sc, zeros(HB - C_ROPE)], -1),
    )


def _pad_cols(w, width):
    return jnp.pad(w, ((0, 0), (0, width - w.shape[1])))


def _layer_weights(w_in, w_uq, w_ukv, w_out, w_rg, b_rg, w_re, b_re):
    pa = (A_HEADS + 2 * A_KV) * HD
    pb = 3 * B_HEADS * HD
    bw = B_HEADS * HD
    g = A_HEADS // A_KV
    qa = w_in[:, :A_HEADS * HD].reshape(D, A_KV, g, HD).transpose(0, 2, 1, 3).reshape(D, A_HEADS * HD)
    kva = w_in[:, A_HEADS * HD:pa]
    qb, kb, vb = (_pad_cols(w_in[:, pa + k * bw:pa + (k + 1) * bw], YW) for k in range(3))
    cq = w_in[:, pa + pb:pa + pb + C_QR]
    ckv = _pad_cols(w_in[:, pa + pb + C_QR:], 2 * C_KVR)
    w_all = jnp.concatenate([qa, kva, qb, kb, vb, cq, ckv], -1).astype(BF16)

    wuq = jnp.pad(w_uq.reshape(C_QR, C_HEADS, C_NOPE + C_ROPE), ((0, 0), (0, 0), (0, HB - C_NOPE - C_ROPE)))
    wuq = wuq.reshape(C_QR, C_QW).astype(BF16)
    ukv = w_ukv.reshape(C_KVR, C_HEADS, C_NOPE + C_V)
    wuk = jnp.pad(ukv[:, :, :C_NOPE], ((0, 0), (0, 0), (0, HB - C_NOPE))).reshape(C_KVR, C_QW).astype(BF16)
    wuvt = ukv[:, :, C_NOPE:].reshape(C_KVR, C_HEADS * C_V).T.astype(BF16)

    oa = w_out[:A_HEADS * HD].reshape(A_KV, g, HD, D).transpose(1, 0, 2, 3).reshape(A_HEADS * HD, D)
    ob = jnp.pad(w_out[A_HEADS * HD:A_HEADS * HD + bw], ((0, YW - bw), (0, 0)))
    oc = jnp.pad(w_out[A_HEADS * HD + bw:], ((C_V, 0), (0, 0)))
    w_parts = tuple(w.astype(BF16) for w in (oa, ob, oc))

    w_rt = jnp.zeros((ROUTER_ROWS, D), F32).at[:N_GROUPS].set(w_rg.T).at[EPG:].set(w_re.T).astype(BF16)
    b_rt = jnp.zeros((ROUTER_ROWS,), F32).at[:N_GROUPS].set(b_rg).at[EPG:].set(b_re)
    b_rt = jnp.broadcast_to(b_rt[:, None], (ROUTER_ROWS, LANE))
    return w_all, dict(wuq=wuq, wuk=wuk, wuvt=wuvt), w_parts, w_rt, b_rt


def kernel(x, c, ctx, c_ctx, w_mod, b_mod, w_in, q_gain_a, k_gain_a, rpb_b, q_lat_gain, kv_lat_gain,
           w_uq, w_ukv, w_out, ln1_g, ln1_b, w_rg, b_rg, w_re, b_re, w1, w3, w2, ln2_g, ln2_b):
    n_batch, seq, _ = x.shape
    ctx_len = ctx.shape[1]
    n_lat, n_ctx = n_batch * seq, n_batch * ctx_len
    assert ctx_len == TQ and seq % TM_IN == 0 and n_ctx % T_ROW == 0 and seq // GRID_W >= NA_BAND
    assert T_ROW == TM_OUT

    consts = _constants()
    tabs = _rope_tables(seq)
    mod_rows = -(-(n_batch + 1) // 8) * 8
    cc = jnp.zeros((mod_rows, D), F32).at[:n_batch].set(c).at[n_batch].set(c_ctx)
    xs = (x.reshape(n_lat, D), ctx.reshape(n_ctx, D))

    for l in range(DEPTH):
        with_ctx = l < DEPTH - 1
        mod = _modulation(cc, w_mod[l], b_mod[l]).reshape(mod_rows, N_MOD, D)
        w_all, mla_w, w_parts, w_rt, b_rt = _layer_weights(w_in[l], w_uq[l], w_ukv[l], w_out[l],
                                                          w_rg[l], b_rg[l], w_re[l], b_re[l])
        qa, kva, qb, kb, vb, cq, ckv = _inproj(xs, mod, w_all, seq, n_lat, n_batch)
        ya = _gqa(qa, kva, tabs, jnp.tile(q_gain_a[l], 2)[None], jnp.tile(k_gain_a[l], 2)[None],
                  consts, seq, ctx_len, n_batch, with_ctx)
        yb = _na(qb, kb, vb, _na_bias(rpb_b[l], seq // GRID_W, seq // TQ), consts["eye"],
                 seq, ctx_len, n_batch, with_ctx)
        yc = _mla(cq, ckv, tabs, q_lat_gain[l][None], kv_lat_gain[l][None], mla_w, consts,
                  seq, ctx_len, n_batch, with_ctx)

        n_rows = n_lat + n_ctx if with_ctx else n_lat
        x1, h2, route, idx, counts = _outproj(ya, yb, yc, xs, mod, w_parts, ln1_g[l][None], ln1_b[l][None],
                                              w_rt, b_rt, consts["tri"], n_rows, seq, n_lat, n_batch)

        cnt = counts[:N_CLS, 0].astype(jnp.int32)
        padded = (cnt + TM_MOE - 1) // TM_MOE * TM_MOE
        ends = jnp.cumsum(padded)
        starts = ends - padded
        n_tiles = n_rows // TM_MOE + N_CLS
        tile_start = jnp.arange(n_tiles, dtype=jnp.int32) * TM_MOE
        tile_cls = jnp.sum((ends[None, :] <= tile_start[:, None]).astype(jnp.int32), -1)
        tile_cls = jnp.minimum(tile_cls, N_CLS - 1)
        tile_valid = (tile_start < ends[-1]).astype(jnp.int32)

        hs = _dispatch(starts, idx, h2, jnp.zeros((n_tiles * TM_MOE, DP), jnp.uint32))
        ys = _moe(consts["cls_lo"][tile_cls], consts["cls_hi"][tile_cls], tile_valid, hs,
                  w1[l].astype(BF16), w3[l].astype(BF16), w2[l].astype(BF16))
        xs = (_combine(starts, idx, ys, x1, route, mod, ln2_g[l][None], ln2_b[l][None], seq, n_lat, n_batch),)

    return xs[0].reshape(n_batch, seq, D)
```

```python
import functools

import numpy as np
import jax
import jax.numpy as jnp
from jax import lax
from jax.experimental import pallas as pl
from jax.experimental.pallas import tpu as pltpu

D = 1024
GRID_W = 64
HD = 64
A_HEADS, A_KV = 6, 2
B_HEADS = 5
NA_R, NA_C = 8, 16
C_HEADS, C_QR, C_KVR, C_NOPE, C_ROPE, C_V = 5, 256, 128, 64, 32, 64
THETA = 10000.0
N_GROUPS, EPG, N_EXP, D_EXP = 4, 8, 32, 256
N_PAIR = EPG * (EPG - 1) // 2
N_CLS = N_GROUPS * N_PAIR
DEPTH = 2
ALPHA = (2 * DEPTH) ** 0.25
EPS = 1e-6
N_MOD = 6

LANE = 128
HB = 2 * HD
YW = 3 * HB

TM_IN = 512
TM_OUT = 512
TQ = 256
TM_MOE = 256
T_ROW = 512
VMEM_LIMIT = 56 * 1024 * 1024

F32 = jnp.float32
BF16 = jnp.bfloat16
NEG_INF = float("-inf")
LOG2E = 1.4426950408889634


def _dot(a, b):
    return jnp.dot(a, b, preferred_element_type=F32)


def _dot_nt(a, b):
    return lax.dot_general(a, b, (((1,), (1,)), ((), ())), preferred_element_type=F32)


def _params(sem, **kw):
    return pltpu.CompilerParams(dimension_semantics=sem, vmem_limit_bytes=VMEM_LIMIT, **kw)


def _ln_rows(x):
    mu = jnp.mean(x, -1, keepdims=True)
    xc = x - mu
    var = jnp.mean(xc * xc, -1, keepdims=True)
    return xc * lax.rsqrt(var + EPS)


def _silu(a):
    return a * (1.0 / (1.0 + jnp.exp(-a)))


DX = D + 128


def _mod_kernel(c_ref, w_ref, b_ref, o_ref):
    s = _silu(c_ref[...])
    o_ref[...] = jnp.dot(s, w_ref[...], preferred_element_type=F32, precision=lax.Precision.HIGHEST) + b_ref[...]


def _modulation(cc, w_mod_l, b_mod_l):
    rows = cc.shape[0]
    tn = 1536
    return pl.pallas_call(
        _mod_kernel,
        out_shape=jax.ShapeDtypeStruct((rows, N_MOD * D), F32),
        grid=(N_MOD * D // tn,),
        in_specs=[
            pl.BlockSpec((rows, D), lambda j: (0, 0)),
            pl.BlockSpec((D, tn), lambda j: (0, j)),
            pl.BlockSpec((1, tn), lambda j: (0, j)),
        ],
        out_specs=pl.BlockSpec((rows, tn), lambda j: (0, j)),
        compiler_params=_params(("arbitrary",)),
        name="modulation",
    )(cc, w_mod_l, b_mod_l.reshape(1, -1))


IN_WIDTHS = (YW, 2 * HB, YW, YW, YW, C_QR, 2 * C_KVR)


def _for_row_source(x_refs, n_lat_tiles, body):
    if len(x_refs) == 1:
        body(x_refs[0])
        return
    i = pl.program_id(0)
    pl.when(i < n_lat_tiles)(lambda: body(x_refs[0]))
    pl.when(i >= n_lat_tiles)(lambda: body(x_refs[1]))


def _inproj_kernel(*refs, n_src, n_lat_tiles):
    x_refs, (mod_ref, w_ref), out_refs = refs[:n_src], refs[n_src:n_src + 2], refs[n_src + 2:]

    def body(x_ref):
        ln = _ln_rows(x_ref[...])
        h = (ln * (1.0 + mod_ref[1:2, :]) + mod_ref[0:1, :]).astype(BF16)
        p = _dot(h, w_ref[...])
        off = 0
        for o_ref, wd in zip(out_refs, IN_WIDTHS):
            o_ref[...] = p[:, off:off + wd].astype(BF16)
            off += wd

    _for_row_source(x_refs, n_lat_tiles, body)


def _mod_index(tile_rows, seq, n_lat, n_batch):
    tiles_per_batch = seq // tile_rows
    n_lat_tiles = n_lat // tile_rows
    return lambda i: (jnp.where(i < n_lat_tiles, i // tiles_per_batch, n_batch), 0, 0)


def _row_source_specs(xs, tile_rows, n_lat):
    if len(xs) == 1:
        return [pl.BlockSpec((tile_rows, D), lambda i: (i, 0))]
    nl = n_lat // tile_rows
    return [pl.BlockSpec((tile_rows, D), lambda i: (jnp.minimum(i, nl - 1), 0)),
            pl.BlockSpec((tile_rows, D), lambda i: (jnp.maximum(i - nl, 0), 0))]


def _inproj(xs, mod, w_all, seq, n_lat, n_batch):
    n = sum(x.shape[0] for x in xs)
    return pl.pallas_call(
        functools.partial(_inproj_kernel, n_src=len(xs), n_lat_tiles=n_lat // TM_IN),
        out_shape=[jax.ShapeDtypeStruct((n, wd), BF16) for wd in IN_WIDTHS],
        grid=(n // TM_IN,),
        in_specs=_row_source_specs(xs, TM_IN, n_lat) + [
            pl.BlockSpec((None, N_MOD, D), _mod_index(TM_IN, seq, n_lat, n_batch)),
            pl.BlockSpec((D, sum(IN_WIDTHS)), lambda i: (0, 0)),
        ],
        out_specs=[pl.BlockSpec((TM_IN, wd), lambda i: (i, 0)) for wd in IN_WIDTHS],
        compiler_params=_params(("arbitrary",)),
        name="inproj",
    )(*xs, mod, w_all)


KEY_CHUNK = 64


def _scores(task):
    qm, key_thunks, _, bias_thunks = task
    chunks = [k() for k in key_thunks]
    s = _dot_nt(jnp.concatenate(chunks, 0), qm())
    parts, r0 = [], 0
    for c, k in enumerate(chunks):
        part = s[r0:r0 + k.shape[0]]
        if bias_thunks is not None and bias_thunks[c] is not None:
            part = part + bias_thunks[c]()
        parts.append(part)
        r0 += k.shape[0]
    return parts


def _softmax(parts):
    blocks = [s[c:c + KEY_CHUNK] for s in parts for c in range(0, s.shape[0], KEY_CHUNK)]
    m = jnp.max(functools.reduce(jnp.maximum, blocks), 0, keepdims=True)
    m_b = jnp.broadcast_to(m, blocks[0].shape)
    acc, p_blocks = None, []
    for blk in blocks:
        p = jnp.exp2(blk - m_b)
        acc = p if acc is None else acc + p
        p_blocks.append(p.astype(BF16))
    return jnp.concatenate(p_blocks, 0), jnp.sum(acc, 0, keepdims=True)


def _attend_heads(tasks):
    outs, scored, soft = [], None, None
    for task in list(tasks) + [None, None]:
        nxt = None if task is None else (_scores(task), task[2])
        nxt_soft = None if scored is None else _softmax(scored[0]) + (scored[1],)
        if soft is not None:
            p, den, vt_thunks = soft
            vt = jnp.concatenate([v() for v in vt_thunks], 1)
            if vt.shape[0] == HB and p.shape[1] == 2 * TQ:
                o = jnp.concatenate([_dot(vt[0:HD], p[:, 0:TQ]), _dot(vt[HD:HB], p[:, TQ:2 * TQ])], 1)
                outs.append(o * (1.0 / den))
            else:
                outs.append(_dot(vt, p) * (1.0 / den))
        scored, soft = nxt, nxt_soft
    return outs


def _row_chunks(ref, n_rows, lanes=slice(None)):
    return [lambda c=c: ref[c * TQ:(c + 1) * TQ, lanes] for c in range(n_rows // TQ)]


def _col_chunks(ref, n_cols, rows):
    return [lambda c=c: ref[rows, c * TQ:(c + 1) * TQ] for c in range(n_cols // TQ)]


def _half_query(q, half):
    first = lax.broadcasted_iota(jnp.int32, q.shape, 1) < HD
    return lambda: jnp.where(first if half == 0 else ~first, q, 0.0).astype(BF16)


def _pair_query(q):
    return lambda: jnp.concatenate([_half_query(q, 0)(), _half_query(q, 1)()], 0)


def _head_pair_rows(lo_t, hi_t):
    return jnp.concatenate([lo_t, hi_t], 0).T


def _pair_rows(o):
    tq = o.shape[1] // 2
    return _head_pair_rows(o[:, 0:tq], o[:, tq:2 * tq])


def _transpose_rows(eye_ref, x):
    return _dot_nt(eye_ref[...], x).astype(BF16)


def _rope_rows(x, cos, sin, rot_ref):
    return x * cos + _dot(x.astype(BF16), rot_ref[...]) * sin


def _attn_specs(n_batch, seq, ctx_len, q_width, kv_width):
    nq = seq // TQ
    ctx0 = n_batch * seq // ctx_len

    def q_idx(b, i):
        return (jnp.where(i < nq, b * nq + i, ctx0 + b), 0)

    return dict(
        q=lambda w: pl.BlockSpec((TQ, w), q_idx),
        lat=lambda w: pl.BlockSpec((seq, w), lambda b, i: (b, 0)),
        ctx=lambda w: pl.BlockSpec((ctx_len, w), lambda b, i: (ctx0 + b, 0)),
        tab=pl.BlockSpec((TQ, HB), lambda b, i: (i, 0)),
        full=lambda shape: pl.BlockSpec(shape, lambda b, i: (0,) * len(shape)),
    )


def _gqa_kernel(q_ref, kvl_ref, kvc_ref, cosq_ref, sinq_ref, cosk_ref, sink_ref, qg_ref, kg_ref,
                ones_ref, rot_ref, eye_ref, o_ref, kl_s, kc_s, vtl_s, vtc_s, *, nq):
    i = pl.program_id(1)

    def head_rms(x):
        ss = _dot((x * x).astype(BF16), ones_ref[...])
        return lax.rsqrt(ss * (1.0 / HD) + EPS)

    @pl.when(i == 0)
    def _():
        kl = kvl_ref[:, 0:HB].astype(F32)
        kn = kl * head_rms(kl) * kg_ref[...]
        kl_s[...] = _rope_rows(kn, cosk_ref[...], sink_ref[...], rot_ref).astype(BF16)
        kc = kvc_ref[:, 0:HB].astype(F32)
        kc_s[...] = (kc * head_rms(kc) * kg_ref[...]).astype(BF16)
        vtl_s[...] = _transpose_rows(eye_ref, kvl_ref[:, HB:2 * HB])
        vtc_s[...] = _transpose_rows(eye_ref, kvc_ref[:, HB:2 * HB])

    seq, ctx_len = kl_s.shape[0], kc_s.shape[0]

    def run(with_latent):
        keys = (_row_chunks(kl_s, seq) if with_latent else []) + _row_chunks(kc_s, ctx_len)
        tasks = []
        for blk in range(A_HEADS // 2):
            q = q_ref[:, blk * HB:(blk + 1) * HB].astype(F32)
            qn = q * head_rms(q) * qg_ref[...]
            qr = _rope_rows(qn, cosq_ref[...], sinq_ref[...], rot_ref) * (HD ** -0.5 * LOG2E)
            vts = (_col_chunks(vtl_s, seq, slice(None)) if with_latent else []) \
                + _col_chunks(vtc_s, ctx_len, slice(None))
            tasks.append((_pair_query(qr), keys, vts, None))
        outs = _attend_heads(tasks)
        for blk in range(A_HEADS // 2):
            o_ref[:, blk * HB:(blk + 1) * HB] = _pair_rows(outs[blk]).astype(BF16)

    @pl.when(i < nq)
    def _():
        run(True)

    @pl.when(i == nq)
    def _():
        run(False)


def _gqa(qa, kva, tabs, q_gain, k_gain, consts, seq, ctx_len, n_batch, with_ctx):
    nq = seq // TQ
    sp = _attn_specs(n_batch, seq, ctx_len, YW, 2 * HB)
    n_out = n_batch * (seq + ctx_len) if with_ctx else n_batch * seq
    return pl.pallas_call(
        functools.partial(_gqa_kernel, nq=nq),
        out_shape=jax.ShapeDtypeStruct((n_out, YW), BF16),
        grid=(n_batch, nq + int(with_ctx)),
        in_specs=[
            sp["q"](YW), sp["lat"](2 * HB), sp["ctx"](2 * HB), sp["tab"], sp["tab"],
            sp["full"]((seq, HB)), sp["full"]((seq, HB)),
            sp["full"]((1, HB)), sp["full"]((1, HB)),
            sp["full"]((HB, HB)), sp["full"]((HB, HB)), sp["full"]((HB, HB)),
        ],
        out_specs=sp["q"](YW),
        scratch_shapes=[pltpu.VMEM((seq, HB), BF16), pltpu.VMEM((ctx_len, HB), BF16),
                        pltpu.VMEM((HB, seq), BF16), pltpu.VMEM((HB, ctx_len), BF16)],
        compiler_params=_params(("arbitrary", "arbitrary")),
        name="gqa",
    )(qa, kva, kva, tabs["cos_a"], tabs["sin_a"], tabs["cos_a"], tabs["sin_a"], q_gain, k_gain,
      consts["ones_a"], consts["rot_a"], consts["eye"])


NA_TILE_ROWS = TQ // GRID_W
NA_BAND = NA_R + NA_TILE_ROWS
NA_KEYS = NA_BAND * GRID_W


def _na_band_start(i, rows):
    return jnp.clip(i * NA_TILE_ROWS - NA_R // 2, 0, rows - NA_BAND)


def _na_kernel(q_ref, kl_ref, vl_ref, kc_ref, vc_ref, bias_ref, eye_ref, o_ref, vtl_s, vtc_s, *, rows, nq):
    i = pl.program_id(1)
    scale = HD ** -0.5 * LOG2E
    band_tiles = NA_KEYS // TQ

    @pl.when(i == 0)
    def _():
        for blk in range(3):
            lanes = slice(blk * HB, (blk + 1) * HB)
            for t in range(nq):
                vtl_s[t, lanes, :] = _transpose_rows(eye_ref, vl_ref[t * TQ:(t + 1) * TQ, lanes])
            vtc_s[lanes, :] = _transpose_rows(eye_ref, vc_ref[:, lanes])

    def run(t0):
        tasks = []
        for blk in range(3):
            paired = 2 * blk + 1 < B_HEADS
            lanes = slice(blk * HB, (blk + 1) * HB)
            hrows = lanes if paired else slice(blk * HB, blk * HB + HD)
            q = q_ref[:, lanes].astype(F32) * scale
            keys = [lambda lanes=lanes: kc_ref[:, lanes]]
            vts = [lambda hrows=hrows: vtc_s[hrows, :]]
            biases = None
            if t0 is not None:
                keys = [lambda j=j, lanes=lanes: kl_ref[pl.ds(pl.multiple_of((t0 + j) * TQ, TQ), TQ), lanes]
                        for j in range(band_tiles)] + keys
                vts = [lambda j=j, hrows=hrows: vtl_s[t0 + j, hrows, :] for j in range(band_tiles)] + vts

                def bias_chunk(j, blk=blk, paired=paired):
                    rows = slice(j * TQ, (j + 1) * TQ)
                    if not paired:
                        return bias_ref[2 * blk, rows, :]
                    return jnp.concatenate([bias_ref[2 * blk, rows, :], bias_ref[2 * blk + 1, rows, :]], 1)

                biases = [functools.partial(bias_chunk, j) for j in range(band_tiles)] + [None]
            tasks.append((_pair_query(q) if paired else _half_query(q, 0), keys, vts, biases))
        outs = _attend_heads(tasks)
        for blk in range(3):
            o = outs[blk]
            y = _pair_rows(o) if o.shape[1] == 2 * TQ else _head_pair_rows(o, jnp.zeros_like(o))
            o_ref[:, blk * HB:(blk + 1) * HB] = y.astype(BF16)

    @pl.when(i < nq)
    def _():
        run(_na_band_start(i, rows) // NA_TILE_ROWS)

    @pl.when(i == nq)
    def _():
        run(None)


def _na_tile_config(i, nq):
    return jnp.where(i == 0, 0, jnp.where(i >= nq - 1, 2, 1))


def _na(qb, kb, vb, bias, eye, seq, ctx_len, n_batch, with_ctx):
    nq = seq // TQ
    sp = _attn_specs(n_batch, seq, ctx_len, YW, YW)
    n_out = n_batch * (seq + ctx_len) if with_ctx else n_batch * seq
    bias_spec = pl.BlockSpec((None,) + bias.shape[1:], lambda b, i: (_na_tile_config(i, nq), 0, 0, 0))
    return pl.pallas_call(
        functools.partial(_na_kernel, rows=seq // GRID_W, nq=nq),
        out_shape=jax.ShapeDtypeStruct((n_out, YW), BF16),
        grid=(n_batch, nq + int(with_ctx)),
        in_specs=[sp["q"](YW), sp["lat"](YW), sp["lat"](YW), sp["ctx"](YW), sp["ctx"](YW), bias_spec,
                  sp["full"]((HB, HB))],
        out_specs=sp["q"](YW),
        scratch_shapes=[pltpu.VMEM((nq, YW, TQ), BF16), pltpu.VMEM((YW, ctx_len), BF16)],
        compiler_params=_params(("arbitrary", "arbitrary")),
        name="na",
    )(qb, kb, vb, kb, vb, bias, eye)


def _na_window_pattern(i, rows):
    r = i * NA_TILE_ROWS + np.arange(NA_TILE_ROWS)
    r0 = np.clip(r - NA_R // 2, 0, rows - NA_R)
    krow = int(np.clip(i * NA_TILE_ROWS - NA_R // 2, 0, rows - NA_BAND)) + np.arange(NA_BAND)
    valid = (krow[None, :] >= r0[:, None]) & (krow[None, :] < r0[:, None] + NA_R)
    dr = np.clip(krow[None, :] - r[:, None] + (NA_R - 1), 0, 2 * NA_R - 2)
    return valid, dr


def _na_bias(rpb, rows, nq):
    col = jnp.arange(GRID_W, dtype=jnp.int32)
    c0 = jnp.clip(col - NA_C // 2, 0, GRID_W - NA_C)
    col_in = (col[None, :] >= c0[:, None]) & (col[None, :] < c0[:, None] + NA_C)
    dc = jnp.clip(col[None, :] - col[:, None] + (NA_C - 1), 0, 2 * NA_C - 2)
    tbl = jnp.where(col_in[None, None], rpb[:, :, dc] * LOG2E, NEG_INF)
    patterns = [_na_window_pattern(i, rows) for i in range(nq)]
    for i in range(2, nq - 1):
        assert all(np.array_equal(a, b) for a, b in zip(patterns[i], patterns[1]))
    out = []
    for valid, dr in (patterns[0], patterns[1], patterns[nq - 1]):
        band = jnp.where(jnp.asarray(valid)[None, :, :, None, None], tbl[:, jnp.asarray(dr)], NEG_INF)
        out.append(band.transpose(0, 2, 4, 1, 3).reshape(B_HEADS, NA_KEYS, TQ))
    return jnp.stack(out).astype(F32)


C_QW = C_HEADS * HB


def _mla_kernel(cq_ref, kvl_ref, kvc_ref, cosq_ref, sinq_ref, cosk_ref, sink_ref, qg_ref, kg_ref,
                wuq_ref, wuk_ref, wuvt_ref, place_ref, rotk_ref, o_ref, kl_s, kc_s, vl_s, vc_s, *, nq):
    i = pl.program_id(1)

    def latent_rms(x, gain):
        ms = jnp.mean(x * x, -1, keepdims=True)
        return (x * lax.rsqrt(ms + EPS) * gain).astype(BF16)

    @pl.when(i == 0)
    def _():
        cl = latent_rms(kvl_ref[:, 0:C_KVR].astype(F32), kg_ref[...])
        rl = _rope_rows(kvl_ref[:, C_KVR:2 * C_KVR].astype(F32), cosk_ref[...], sink_ref[...], rotk_ref)
        kl_s[...] = (_dot(cl, wuk_ref[...]) + _dot(rl.astype(BF16), place_ref[...])).astype(BF16)
        vl_s[...] = _dot_nt(wuvt_ref[...], cl).astype(BF16)
        cc = latent_rms(kvc_ref[:, 0:C_KVR].astype(F32), kg_ref[...])
        kc_s[...] = (_dot(cc, wuk_ref[...]) + _dot(kvc_ref[:, C_KVR:2 * C_KVR], place_ref[...])).astype(BF16)
        vc_s[...] = _dot_nt(wuvt_ref[...], cc).astype(BF16)

    seq, ctx_len = kl_s.shape[0], kc_s.shape[0]

    def run(with_latent):
        cq = latent_rms(cq_ref[...].astype(F32), qg_ref[...])
        qq = _dot(cq, wuq_ref[...])
        tasks = []
        for h in range(C_HEADS):
            lanes = slice(h * HB, (h + 1) * HB)
            q, q_rot = qq[:, lanes], qq[:, C_QW + h * HB:C_QW + (h + 1) * HB]
            qr = (q * cosq_ref[...] + q_rot * sinq_ref[...]) * ((C_NOPE + C_ROPE) ** -0.5 * LOG2E)
            hrows = slice(h * C_V, (h + 1) * C_V)
            keys, vts = _row_chunks(kc_s, ctx_len, lanes), _col_chunks(vc_s, ctx_len, hrows)
            if with_latent:
                keys, vts = _row_chunks(kl_s, seq, lanes) + keys, _col_chunks(vl_s, seq, hrows) + vts
            tasks.append((lambda qr=qr: qr.astype(BF16), keys, vts, None))
        outs = _attend_heads(tasks)
        outs = [jnp.zeros_like(outs[0])] + outs
        for blk in range(3):
            o_ref[:, blk * HB:(blk + 1) * HB] = _head_pair_rows(outs[2 * blk], outs[2 * blk + 1]).astype(BF16)

    @pl.when(i < nq)
    def _():
        run(True)

    @pl.when(i == nq)
    def _():
        run(False)


def _mla(cq, ckv, tabs, q_gain, kv_gain, wts, consts, seq, ctx_len, n_batch, with_ctx):
    nq = seq // TQ
    sp = _attn_specs(n_batch, seq, ctx_len, C_QR, 2 * C_KVR)
    full = sp["full"]
    n_out = n_batch * (seq + ctx_len) if with_ctx else n_batch * seq
    return pl.pallas_call(
        functools.partial(_mla_kernel, nq=nq),
        out_shape=jax.ShapeDtypeStruct((n_out, YW), BF16),
        grid=(n_batch, nq + int(with_ctx)),
        in_specs=[
            sp["q"](C_QR), sp["lat"](2 * C_KVR), sp["ctx"](2 * C_KVR), sp["tab"], sp["tab"],
            full((seq, HB)), full((seq, HB)), full((1, C_QR)), full((1, C_KVR)),
            full((C_QR, 2 * C_QW)), full((C_KVR, C_QW)), full((C_HEADS * C_V, C_KVR)),
            full((HB, C_QW)), full((HB, HB)),
        ],
        out_specs=sp["q"](YW),
        scratch_shapes=[pltpu.VMEM((seq, C_QW), BF16), pltpu.VMEM((ctx_len, C_QW), BF16),
                        pltpu.VMEM((C_HEADS * C_V, seq), BF16), pltpu.VMEM((C_HEADS * C_V, ctx_len), BF16)],
        compiler_params=_params(("arbitrary", "arbitrary")),
        name="mla",
    )(cq, ckv, ckv, tabs["cos_cq"], tabs["sin_cq"], tabs["cos_ck"], tabs["sin_ck"], q_gain, kv_gain,
      wts["wuq"], wts["wuk"], wts["wuvt"], consts["place_c"], consts["rot_ck"])


ROUTER_ROWS = (N_GROUPS + 1) * EPG


def _outproj_kernel(*refs, n_src, n_lat_tiles):
    x_refs = refs[:n_src]
    (ya_ref, yb_ref, yc_ref, mod_ref, wa_ref, wb_ref, wc_ref, g_ref, b_ref, wrt_ref, brt_ref, tri_ref,
     x1_ref, h2_ref, idx_ref, cnt_ref, carry_s) = refs[n_src:]
    i = pl.program_id(0)

    @pl.when(i == 0)
    def _():
        carry_s[...] = jnp.zeros_like(carry_s)

    y = _dot(ya_ref[...], wa_ref[...]) + _dot(yb_ref[...], wb_ref[...]) + _dot(yc_ref[...], wc_ref[...])

    def residual(x_ref):
        x1_ref[...] = _ln_rows(ALPHA * x_ref[...] + mod_ref[2:3, :] * y) * g_ref[...] + b_ref[...]

    _for_row_source(x_refs, n_lat_tiles, residual)
    h2 = _ln_rows(x1_ref[...]) * (1.0 + mod_ref[4:5, :]) + mod_ref[3:4, :]
    h2_ref[:, 0:D] = h2

    logit = _dot_nt(wrt_ref[...], h2.astype(BF16)) + brt_ref[:, 0:1]
    sub = lax.broadcasted_iota(jnp.int32, (EPG, TM_OUT), 0)

    def col_max(v):
        return jnp.max(v, 0, keepdims=True)

    def first_row(mask):
        return jnp.min(jnp.where(mask, sub, EPG), 0, keepdims=True)

    g_logit = jnp.where(sub < N_GROUPS, logit[0:EPG], NEG_INF)
    g_max = col_max(g_logit)
    g_idx = first_row(g_logit == g_max)
    g_w = 1.0 / jnp.sum(jnp.exp(g_logit - g_max), 0, keepdims=True)
    e_logit = logit[EPG:2 * EPG]
    for g in range(1, N_GROUPS):
        e_logit = jnp.where(g_idx == g, logit[(g + 1) * EPG:(g + 2) * EPG], e_logit)
    m1 = col_max(e_logit)
    i1 = first_row(e_logit == m1)
    rest = jnp.where(sub == i1, NEG_INF, e_logit)
    m2 = col_max(rest)
    i2 = first_row(rest == m2)
    t = jnp.exp(m2 - m1)
    w1 = g_w / (1.0 + t)
    w2 = g_w * t / (1.0 + t)
    lo, hi = jnp.minimum(i1, i2), jnp.maximum(i1, i2)
    pair = lax.shift_right_logical(lo * (2 * EPG - 1 - lo), jnp.ones_like(lo)) + hi - lo - 1
    cls = g_idx * N_PAIR + pair
    w_lo = jnp.where(i1 < i2, w1, w2)
    w_hi = jnp.where(i1 < i2, w2, w1)

    cls_row = lax.broadcasted_iota(jnp.int32, (LANE, TM_OUT), 0)
    onehot = (cls_row == cls).astype(F32)
    prefix = _dot(onehot.astype(BF16), tri_ref[...]) + carry_s[:, 0:1]
    rank = jnp.sum(onehot * prefix, 0, keepdims=True)
    carry_s[...] = carry_s[...] + jnp.sum(onehot, 1, keepdims=True)
    cnt_ref[...] = carry_s[...]

    idx_ref[...] = jnp.where(sub == 0, cls, jnp.where(sub == 1, rank.astype(jnp.int32), 0))
    rows = jnp.where(sub == 0, w_lo, jnp.where(sub == 1, w_hi, 0.0))
    h2_ref[:, D:DX] = jnp.concatenate([rows, jnp.zeros((LANE - EPG, TM_OUT), F32)], 0).T


def _outproj(ya, yb, yc, xs, mod, w_parts, ln_g, ln_b, w_rt, b_rt, tri, n_rows, seq, n_lat, n_batch):
    row = lambda w: pl.BlockSpec((TM_OUT, w), lambda i: (i, 0))
    full = lambda shape: pl.BlockSpec(shape, lambda i: (0, 0))
    n_tiles = n_rows // TM_OUT
    return pl.pallas_call(
        functools.partial(_outproj_kernel, n_src=len(xs), n_lat_tiles=n_lat // TM_OUT),
        out_shape=[
            jax.ShapeDtypeStruct((n_rows, D), F32),
            jax.ShapeDtypeStruct((n_rows, DX), F32),
            jax.ShapeDtypeStruct((n_tiles, EPG, TM_OUT), jnp.int32),
            jax.ShapeDtypeStruct((LANE, LANE), F32),
        ],
        grid=(n_tiles,),
        in_specs=_row_source_specs(xs, TM_OUT, n_lat) + [
            row(YW), row(YW), row(YW),
            pl.BlockSpec((None, N_MOD, D), _mod_index(TM_OUT, seq, n_lat, n_batch)),
            full((YW, D)), full((YW, D)), full((YW, D)),
            full((1, D)), full((1, D)), full((ROUTER_ROWS, D)), full((ROUTER_ROWS, LANE)), full((TM_OUT, TM_OUT)),
        ],
        out_specs=[row(D), row(DX), pl.BlockSpec((None, EPG, TM_OUT), lambda i: (i, 0, 0)),
                   full((LANE, LANE))],
        scratch_shapes=[pltpu.VMEM((LANE, LANE), F32)],
        compiler_params=_params(("arbitrary",)),
        name="outproj",
    )(*xs, ya, yb, yc, mod, *w_parts, ln_g, ln_b, w_rt, b_rt, tri)


SUB = 8


def _wait_rows(make_copy):
    def wait(u, c):
        for k in range(SUB):
            make_copy(0, 0, 0).wait()
        return c

    lax.fori_loop(0, T_ROW // SUB, wait, 0)


def _permute_rows(starts_ref, idx_hbm, idx_s, sem_idx, make_copy, n=None, drain=True):
    i = pl.program_id(0)
    n = pl.num_programs(0) if n is None else n

    def index_copies(step, slot):
        return [pltpu.make_async_copy(idx_hbm.at[step, k], idx_s[2 * slot + k], sem_idx.at[slot, k])
                for k in range(2)]

    @pl.when(i == 0)
    def _():
        for cp in index_copies(0, 0):
            cp.start()

    for slot in range(2):

        @pl.when(lax.rem(i, 2) == slot)
        def _(slot=slot):
            for cp in index_copies(i, slot):
                cp.wait()

            @pl.when(i + 1 < n)
            def _():
                for cp in index_copies(i + 1, 1 - slot):
                    cp.start()

            cls_s, rank_s = idx_s[2 * slot], idx_s[2 * slot + 1]
            copy = make_copy(slot)

            def start(u, c):
                for k in range(SUB):
                    t = u * SUB + k
                    copy(u, k, starts_ref[cls_s[t]] + rank_s[t]).start(priority=k % 2)
                return c

            lax.fori_loop(0, T_ROW // SUB, start, 0)
            if drain:
                _wait_rows(copy)


def _dispatch_kernel(starts_ref, idx_hbm, h_ref, hs_in, hs_out, *scratch):
    del hs_in
    idx_s, sem_idx, sem_rows = scratch[:4], scratch[4], scratch[5]
    _permute_rows(starts_ref, idx_hbm, idx_s, sem_idx,
                  lambda slot: lambda u, k, pos: pltpu.make_async_copy(
                      h_ref.at[u, pl.ds(k, 1)], hs_out.at[pl.ds(pos, 1)], sem_rows))


_ROW_IDX_SCRATCH = [pltpu.SMEM((T_ROW,), jnp.int32)] * 4 + [pltpu.SemaphoreType.DMA((2, 2))]


def _dispatch(starts, idx, h2, hs0):
    any_spec = pl.BlockSpec(memory_space=pl.ANY)
    return pl.pallas_call(
        _dispatch_kernel,
        out_shape=jax.ShapeDtypeStruct(hs0.shape, hs0.dtype),
        grid_spec=pltpu.PrefetchScalarGridSpec(
            num_scalar_prefetch=1,
            grid=(idx.shape[0],),
            in_specs=[any_spec, pl.BlockSpec((T_ROW // SUB, SUB, DX), lambda i, st: (i, 0, 0)), any_spec],
            out_specs=any_spec,
            scratch_shapes=_ROW_IDX_SCRATCH + [pltpu.SemaphoreType.DMA],
        ),
        input_output_aliases={3: 0},
        compiler_params=_params(("arbitrary",)),
        name="dispatch",
    )(starts, idx, h2.reshape(-1, SUB, DX), hs0)


def _moe_kernel(elo_ref, ehi_ref, valid_ref, h_ref, w1a, w3a, w2a, w1b, w3b, w2b, o_ref):
    j = pl.program_id(0)

    def ffn(h, w1, w3, w2):
        hid = (_silu(_dot(h, w1[...])) * _dot(h, w3[...])).astype(BF16)
        return _dot(hid, w2[...])

    @pl.when(valid_ref[j] != 0)
    def _():
        h = h_ref[:, 0:D].astype(BF16)
        w_lo, w_hi = h_ref[:, D:D + 1], h_ref[:, D + 1:D + 2]
        o_ref[...] = w_lo * ffn(h, w1a, w3a, w2a) + w_hi * ffn(h, w1b, w3b, w2b)

    @pl.when(valid_ref[j] == 0)
    def _():
        o_ref[...] = jnp.zeros_like(o_ref)


def _moe(tile_elo, tile_ehi, tile_valid, hs, w1, w3, w2):
    lo_spec = lambda shape: pl.BlockSpec((None,) + shape, lambda j, elo, ehi, v: (elo[j], 0, 0))
    hi_spec = lambda shape: pl.BlockSpec((None,) + shape, lambda j, elo, ehi, v: (ehi[j], 0, 0))
    up, down = (D, D_EXP), (D_EXP, D)
    return pl.pallas_call(
        _moe_kernel,
        out_shape=jax.ShapeDtypeStruct((hs.shape[0], D), F32),
        grid_spec=pltpu.PrefetchScalarGridSpec(
            num_scalar_prefetch=3,
            grid=(hs.shape[0] // TM_MOE,),
            in_specs=[
                pl.BlockSpec((TM_MOE, DX), lambda j, elo, ehi, v: (j, 0)),
                lo_spec(up), lo_spec(up), lo_spec(down), hi_spec(up), hi_spec(up), hi_spec(down),
            ],
            out_specs=pl.BlockSpec((TM_MOE, D), lambda j, elo, ehi, v: (j, 0)),
        ),
        compiler_params=_params(("arbitrary",)),
        name="moe_ffn",
    )(tile_elo, tile_ehi, tile_valid, hs, w1, w3, w2, w1, w3, w2)


def _combine_kernel(starts_ref, idx_hbm, ys_hbm, x1_ref, mod_ref, g_ref, b_ref, o_ref, *scratch):
    idx_s, sem_idx, buf, sem_rows = scratch[:4], scratch[4], scratch[5], scratch[6]
    i = pl.program_id(0)
    n_tiles = pl.num_programs(0) - 1

    def gather(slot):
        return lambda u, k, pos: pltpu.make_async_copy(ys_hbm.at[pl.ds(pos, 1)], buf.at[slot, u, pl.ds(k, 1)],
                                                       sem_rows.at[slot])

    @pl.when(i < n_tiles)
    def _():
        _permute_rows(starts_ref, idx_hbm, idx_s, sem_idx, gather, n=n_tiles, drain=False)

    for slot in range(2):

        @pl.when((i > 0) & (lax.rem(i + 1, 2) == slot))
        def _(slot=slot):
            _wait_rows(gather(slot))
            moe = buf[slot].reshape(T_ROW, D)
            o_ref[...] = _ln_rows(ALPHA * x1_ref[...] + mod_ref[5:6, :] * moe) * g_ref[...] + b_ref[...]


def _combine(starts, idx, ys, x1, mod, ln_g, ln_b, seq, n_lat, n_batch):
    n_rows = x1.shape[0]
    any_spec = pl.BlockSpec(memory_space=pl.ANY)
    mod_idx = _mod_index(T_ROW, seq, n_lat, n_batch)
    tile = lambda i: jnp.maximum(i - 1, 0)
    return pl.pallas_call(
        _combine_kernel,
        out_shape=jax.ShapeDtypeStruct((n_rows, D), F32),
        grid_spec=pltpu.PrefetchScalarGridSpec(
            num_scalar_prefetch=1,
            grid=(n_rows // T_ROW + 1,),
            in_specs=[
                any_spec, any_spec,
                pl.BlockSpec((T_ROW, D), lambda i, st: (tile(i), 0)),
                pl.BlockSpec((None, N_MOD, D), lambda i, st: mod_idx(tile(i))),
                pl.BlockSpec((1, D), lambda i, st: (0, 0)), pl.BlockSpec((1, D), lambda i, st: (0, 0)),
            ],
            out_specs=pl.BlockSpec((T_ROW, D), lambda i, st: (tile(i), 0)),
            scratch_shapes=_ROW_IDX_SCRATCH + [pltpu.VMEM((2, T_ROW // SUB, SUB, D), F32),
                                               pltpu.SemaphoreType.DMA((2,))],
        ),
        compiler_params=_params(("arbitrary",)),
        name="combine",
    )(starts, idx, ys, x1, mod, ln_g, ln_b)


def _rot_matrix(width, start, half):
    r = np.zeros((width, width), np.float32)
    for j in range(half):
        r[start + half + j, start + j] = -1.0
        r[start + j, start + half + j] = 1.0
    return r


def _constants():
    ones_a = np.kron(np.eye(2, dtype=np.float32), np.ones((HD, HD), np.float32))
    rot_a = _rot_matrix(HB, 0, HD // 2) + _rot_matrix(HB, HD, HD // 2)
    rot_ck = _rot_matrix(HB, 0, C_ROPE // 2)
    place = np.zeros((HB, C_QW), np.float32)
    for h in range(C_HEADS):
        for j in range(C_ROPE):
            place[j, h * HB + C_NOPE + j] = 1.0
    tri = np.triu(np.ones((TM_OUT, TM_OUT), np.float32), 1)
    cls_lo, cls_hi = [], []
    for g in range(N_GROUPS):
        for lo in range(EPG):
            for hi in range(lo + 1, EPG):
                cls_lo.append(g * EPG + lo)
                cls_hi.append(g * EPG + hi)
    as_bf = lambda a: jnp.asarray(a, BF16)
    return dict(ones_a=as_bf(ones_a), rot_a=as_bf(rot_a), rot_ck=as_bf(rot_ck),
                place_c=as_bf(place), tri=as_bf(tri), eye=as_bf(np.eye(HB, dtype=np.float32)),
                cls_lo=jnp.asarray(cls_lo, jnp.int32), cls_hi=jnp.asarray(cls_hi, jnp.int32))


def _rope_tables(seq):
    t = jnp.arange(seq, dtype=jnp.int32)
    row = (t // GRID_W).astype(F32)
    col = (t % GRID_W).astype(F32)

    def angles(dim):
        n_freq = dim // 4
        inv = THETA ** (-jnp.arange(n_freq, dtype=F32) / n_freq)
        ang = jnp.concatenate([row[:, None] * inv, col[:, None] * inv], -1)
        return jnp.concatenate([ang, jnp.zeros((TQ, dim // 2), F32)], 0)

    ang_a, ang_c = angles(HD), angles(C_ROPE)
    ones = lambda w: jnp.ones((seq + TQ, w), F32)
    zeros = lambda w: jnp.zeros((seq + TQ, w), F32)
    ca, sa = jnp.cos(ang_a), jnp.sin(ang_a)
    cc, sc = jnp.cos(ang_c), jnp.sin(ang_c)
    return dict(
        cos_a=jnp.tile(ca, (1, 4)), sin_a=jnp.tile(sa, (1, 4)),
        cos_cq=jnp.concatenate([ones(C_NOPE), cc, cc, ones(HB - C_NOPE - C_ROPE)], -1),
        sin_cq=jnp.concatenate([zeros(C_NOPE), sc, sc, zeros(HB - C_NOPE - C_ROPE)], -1),
        cos_ck=jnp.concatenate([cc, cc, ones(HB - C_ROPE)], -1),
        sin_ck=jnp.concatenate([sc, sc, zeros(HB - C_ROPE)], -1),
    )


def _pad_cols(w, width):
    return jnp.pad(w, ((0, 0), (0, width - w.shape[1])))


def _layer_weights(w_in, w_uq, w_ukv, w_out, w_rg, b_rg, w_re, b_re):
    pa = (A_HEADS + 2 * A_KV) * HD
    pb = 3 * B_HEADS * HD
    bw = B_HEADS * HD
    g = A_HEADS // A_KV
    qa = w_in[:, :A_HEADS * HD].reshape(D, A_KV, g, HD).transpose(0, 2, 1, 3).reshape(D, A_HEADS * HD)
    kva = w_in[:, A_HEADS * HD:pa]
    qb, kb, vb = (_pad_cols(w_in[:, pa + k * bw:pa + (k + 1) * bw], YW) for k in range(3))
    cq = w_in[:, pa + pb:pa + pb + C_QR]
    ckv = _pad_cols(w_in[:, pa + pb + C_QR:], 2 * C_KVR)
    w_all = jnp.concatenate([qa, kva, qb, kb, vb, cq, ckv], -1).astype(BF16)

    wuq = jnp.pad(w_uq.reshape(C_QR, C_HEADS, C_NOPE + C_ROPE), ((0, 0), (0, 0), (0, HB - C_NOPE - C_ROPE)))
    x1, x2 = wuq[:, :, C_NOPE:C_NOPE + C_ROPE // 2], wuq[:, :, C_NOPE + C_ROPE // 2:C_NOPE + C_ROPE]
    wuq_rot = jnp.concatenate([jnp.zeros_like(wuq[:, :, :C_NOPE]), -x2, x1,
                               jnp.zeros_like(wuq[:, :, C_NOPE + C_ROPE:])], -1)
    wuq = jnp.concatenate([wuq.reshape(C_QR, C_QW), wuq_rot.reshape(C_QR, C_QW)], -1).astype(BF16)
    ukv = w_ukv.reshape(C_KVR, C_HEADS, C_NOPE + C_V)
    wuk = jnp.pad(ukv[:, :, :C_NOPE], ((0, 0), (0, 0), (0, HB - C_NOPE))).reshape(C_KVR, C_QW).astype(BF16)
    wuvt = ukv[:, :, C_NOPE:].reshape(C_KVR, C_HEADS * C_V).T.astype(BF16)

    oa = w_out[:A_HEADS * HD].reshape(A_KV, g, HD, D).transpose(1, 0, 2, 3).reshape(A_HEADS * HD, D)
    ob = jnp.pad(w_out[A_HEADS * HD:A_HEADS * HD + bw], ((0, YW - bw), (0, 0)))
    oc = jnp.pad(w_out[A_HEADS * HD + bw:], ((C_V, 0), (0, 0)))
    w_parts = tuple(w.astype(BF16) for w in (oa, ob, oc))

    w_rt = jnp.zeros((ROUTER_ROWS, D), F32).at[:N_GROUPS].set(w_rg.T).at[EPG:].set(w_re.T).astype(BF16)
    b_rt = jnp.zeros((ROUTER_ROWS,), F32).at[:N_GROUPS].set(b_rg).at[EPG:].set(b_re)
    b_rt = jnp.broadcast_to(b_rt[:, None], (ROUTER_ROWS, LANE))
    return w_all, dict(wuq=wuq, wuk=wuk, wuvt=wuvt), w_parts, w_rt, b_rt


def kernel(x, c, ctx, c_ctx, w_mod, b_mod, w_in, q_gain_a, k_gain_a, rpb_b, q_lat_gain, kv_lat_gain,
           w_uq, w_ukv, w_out, ln1_g, ln1_b, w_rg, b_rg, w_re, b_re, w1, w3, w2, ln2_g, ln2_b):
    n_batch, seq, _ = x.shape
    ctx_len = ctx.shape[1]
    n_lat, n_ctx = n_batch * seq, n_batch * ctx_len
    assert ctx_len == TQ and seq % TM_IN == 0 and n_ctx % T_ROW == 0 and seq // GRID_W >= NA_BAND
    assert T_ROW == TM_OUT

    consts = _constants()
    tabs = _rope_tables(seq)
    mod_rows = -(-(n_batch + 1) // 8) * 8
    cc = jnp.zeros((mod_rows, D), F32).at[:n_batch].set(c).at[n_batch].set(c_ctx)
    xs = (x.reshape(n_lat, D), ctx.reshape(n_ctx, D))

    for l in range(DEPTH):
        with_ctx = l < DEPTH - 1
        mod = _modulation(cc, w_mod[l], b_mod[l]).reshape(mod_rows, N_MOD, D)
        w_all, mla_w, w_parts, w_rt, b_rt = _layer_weights(w_in[l], w_uq[l], w_ukv[l], w_out[l],
                                                          w_rg[l], b_rg[l], w_re[l], b_re[l])
        qa, kva, qb, kb, vb, cq, ckv = _inproj(xs, mod, w_all, seq, n_lat, n_batch)
        ya = _gqa(qa, kva, tabs, jnp.tile(q_gain_a[l], 2)[None], jnp.tile(k_gain_a[l], 2)[None],
                  consts, seq, ctx_len, n_batch, with_ctx)
        yb = _na(qb, kb, vb, _na_bias(rpb_b[l], seq // GRID_W, seq // TQ), consts["eye"],
                 seq, ctx_len, n_batch, with_ctx)
        yc = _mla(cq, ckv, tabs, q_lat_gain[l][None], kv_lat_gain[l][None], mla_w, consts,
                  seq, ctx_len, n_batch, with_ctx)

        n_rows = n_lat + n_ctx if with_ctx else n_lat
        x1, h2, idx, counts = _outproj(ya, yb, yc, xs, mod, w_parts, ln1_g[l][None], ln1_b[l][None],
                                       w_rt, b_rt, consts["tri"], n_rows, seq, n_lat, n_batch)

        cnt = counts[:N_CLS, 0].astype(jnp.int32)
        padded = (cnt + TM_MOE - 1) // TM_MOE * TM_MOE
        ends = jnp.cumsum(padded)
        starts = ends - padded
        n_tiles = n_rows // TM_MOE + N_CLS
        tile_start = jnp.arange(n_tiles, dtype=jnp.int32) * TM_MOE
        tile_cls = jnp.sum((ends[None, :] <= tile_start[:, None]).astype(jnp.int32), -1)
        tile_cls = jnp.minimum(tile_cls, N_CLS - 1)
        tile_valid = (tile_start < ends[-1]).astype(jnp.int32)

        hs = _dispatch(starts, idx, h2, jnp.zeros((n_tiles * TM_MOE, DX), F32))
        ys = _moe(consts["cls_lo"][tile_cls], consts["cls_hi"][tile_cls], tile_valid, hs,
                  w1[l].astype(BF16), w3[l].astype(BF16), w2[l].astype(BF16))
        xs = (_combine(starts, idx, ys, x1, mod, ln2_g[l][None], ln2_b[l][None], seq, n_lat, n_batch),)

    return xs[0].reshape(n_batch, seq, D)
```

```python
import functools

import numpy as np
import jax
import jax.numpy as jnp
from jax import lax
from jax.experimental import pallas as pl
from jax.experimental.pallas import tpu as pltpu

D = 1024
GRID_W = 64
HD = 64
A_HEADS, A_KV = 6, 2
B_HEADS = 5
NA_R, NA_C = 8, 16
C_HEADS, C_QR, C_KVR, C_NOPE, C_ROPE, C_V = 5, 256, 128, 64, 32, 64
THETA = 10000.0
N_GROUPS, EPG, N_EXP, D_EXP = 4, 8, 32, 256
N_PAIR = EPG * (EPG - 1) // 2
N_CLS = N_GROUPS * N_PAIR
DEPTH = 2
ALPHA = (2 * DEPTH) ** 0.25
EPS = 1e-6
N_MOD = 6

LANE = 128
HB = 2 * HD
YW = 3 * HB

TM_IN = 512
TM_OUT = 512
TQ = 256
TM_MOE = 256
T_ROW = 512
VMEM_LIMIT = 56 * 1024 * 1024

F32 = jnp.float32
BF16 = jnp.bfloat16
NEG_INF = float("-inf")
LOG2E = 1.4426950408889634


def _dot(a, b):
    return jnp.dot(a, b, preferred_element_type=F32)


def _dot_nt(a, b):
    return lax.dot_general(a, b, (((1,), (1,)), ((), ())), preferred_element_type=F32)


def _params(sem, **kw):
    return pltpu.CompilerParams(dimension_semantics=sem, vmem_limit_bytes=VMEM_LIMIT, **kw)


def _ln_rows(x):
    mu = jnp.mean(x, -1, keepdims=True)
    xc = x - mu
    var = jnp.mean(xc * xc, -1, keepdims=True)
    return xc * lax.rsqrt(var + EPS)


def _silu(a):
    return a * (1.0 / (1.0 + jnp.exp(-a)))


DX = D + LANE


def _mod_kernel(c_ref, w_ref, b_ref, o_ref):
    s = _silu(c_ref[...])
    o_ref[...] = jnp.dot(s, w_ref[...], preferred_element_type=F32, precision=lax.Precision.HIGHEST) + b_ref[...]


def _modulation(cc, w_mod_l, b_mod_l):
    rows = cc.shape[0]
    tn = 1536
    return pl.pallas_call(
        _mod_kernel,
        out_shape=jax.ShapeDtypeStruct((rows, N_MOD * D), F32),
        grid=(N_MOD * D // tn,),
        in_specs=[
            pl.BlockSpec((rows, D), lambda j: (0, 0)),
            pl.BlockSpec((D, tn), lambda j: (0, j)),
            pl.BlockSpec((1, tn), lambda j: (0, j)),
        ],
        out_specs=pl.BlockSpec((rows, tn), lambda j: (0, j)),
        compiler_params=_params(("arbitrary",)),
        name="modulation",
    )(cc, w_mod_l, b_mod_l.reshape(1, -1))


IN_WIDTHS = (YW, 2 * HB, YW, YW, YW, C_QR, 2 * C_KVR)


def _for_row_source(x_refs, n_lat_tiles, body):
    if len(x_refs) == 1:
        body(x_refs[0])
        return
    i = pl.program_id(0)
    pl.when(i < n_lat_tiles)(lambda: body(x_refs[0]))
    pl.when(i >= n_lat_tiles)(lambda: body(x_refs[1]))


def _inproj_kernel(*refs, n_src, n_lat_tiles):
    x_refs, (mod_ref, w_ref), out_refs = refs[:n_src], refs[n_src:n_src + 2], refs[n_src + 2:]

    def body(x_ref):
        ln = _ln_rows(x_ref[...])
        h = (ln * (1.0 + mod_ref[1:2, :]) + mod_ref[0:1, :]).astype(BF16)
        p = _dot(h, w_ref[...])
        off = 0
        for o_ref, wd in zip(out_refs, IN_WIDTHS):
            o_ref[...] = p[:, off:off + wd].astype(BF16)
            off += wd

    _for_row_source(x_refs, n_lat_tiles, body)


def _mod_index(tile_rows, seq, n_lat, n_batch):
    tiles_per_batch = seq // tile_rows
    n_lat_tiles = n_lat // tile_rows
    return lambda i: (jnp.where(i < n_lat_tiles, i // tiles_per_batch, n_batch), 0, 0)


def _row_source_specs(xs, tile_rows, n_lat):
    if len(xs) == 1:
        return [pl.BlockSpec((tile_rows, D), lambda i: (i, 0))]
    nl = n_lat // tile_rows
    return [pl.BlockSpec((tile_rows, D), lambda i: (jnp.minimum(i, nl - 1), 0)),
            pl.BlockSpec((tile_rows, D), lambda i: (jnp.maximum(i - nl, 0), 0))]


def _inproj(xs, mod, w_all, seq, n_lat, n_batch):
    n = sum(x.shape[0] for x in xs)
    return pl.pallas_call(
        functools.partial(_inproj_kernel, n_src=len(xs), n_lat_tiles=n_lat // TM_IN),
        out_shape=[jax.ShapeDtypeStruct((n, wd), BF16) for wd in IN_WIDTHS],
        grid=(n // TM_IN,),
        in_specs=_row_source_specs(xs, TM_IN, n_lat) + [
            pl.BlockSpec((None, N_MOD, D), _mod_index(TM_IN, seq, n_lat, n_batch)),
            pl.BlockSpec((D, sum(IN_WIDTHS)), lambda i: (0, 0)),
        ],
        out_specs=[pl.BlockSpec((TM_IN, wd), lambda i: (i, 0)) for wd in IN_WIDTHS],
        compiler_params=_params(("arbitrary",)),
        name="inproj",
    )(*xs, mod, w_all)


KEY_CHUNK = 64


def _scores(task):
    qm, key_thunks, _, bias_thunks = task
    chunks = [k() for k in key_thunks]
    s = _dot_nt(jnp.concatenate(chunks, 0), qm())
    parts, r0 = [], 0
    for c, k in enumerate(chunks):
        part = s[r0:r0 + k.shape[0]]
        if bias_thunks is not None and bias_thunks[c] is not None:
            part = part + bias_thunks[c]()
        parts.append(part)
        r0 += k.shape[0]
    return parts


def _softmax(parts):
    blocks = [s[c:c + KEY_CHUNK] for s in parts for c in range(0, s.shape[0], KEY_CHUNK)]
    m = jnp.max(functools.reduce(jnp.maximum, blocks), 0, keepdims=True)
    m_b = jnp.broadcast_to(m, blocks[0].shape)
    acc, p_blocks = None, []
    for blk in blocks:
        p = jnp.exp2(blk - m_b)
        acc = p if acc is None else acc + p
        p_blocks.append(p.astype(BF16))
    return jnp.concatenate(p_blocks, 0), jnp.sum(acc, 0, keepdims=True)


def _attend_heads(tasks):
    outs, scored, soft = [], None, None
    for task in list(tasks) + [None, None]:
        nxt = None if task is None else (_scores(task), task[2])
        nxt_soft = None if scored is None else _softmax(scored[0]) + (scored[1],)
        if soft is not None:
            p, den, vt_thunks = soft
            vt = jnp.concatenate([v() for v in vt_thunks], 1)
            if vt.shape[0] == HB and p.shape[1] == 2 * TQ:
                o = jnp.concatenate([_dot(vt[0:HD], p[:, 0:TQ]), _dot(vt[HD:HB], p[:, TQ:2 * TQ])], 1)
                outs.append(o * (1.0 / den))
            else:
                outs.append(_dot(vt, p) * (1.0 / den))
        scored, soft = nxt, nxt_soft
    return outs


def _row_chunks(ref, n_rows, lanes=slice(None)):
    return [lambda c=c: ref[c * TQ:(c + 1) * TQ, lanes] for c in range(n_rows // TQ)]


def _col_chunks(ref, n_cols, rows):
    return [lambda c=c: ref[rows, c * TQ:(c + 1) * TQ] for c in range(n_cols // TQ)]


def _half_query(q, half):
    first = lax.broadcasted_iota(jnp.int32, q.shape, 1) < HD
    return lambda: jnp.where(first if half == 0 else ~first, q, 0.0).astype(BF16)


def _pair_query(q):
    return lambda: jnp.concatenate([_half_query(q, 0)(), _half_query(q, 1)()], 0)


def _head_pair_rows(lo_t, hi_t):
    return jnp.concatenate([lo_t, hi_t], 0).T


def _pair_rows(o):
    tq = o.shape[1] // 2
    return _head_pair_rows(o[:, 0:tq], o[:, tq:2 * tq])


def _transpose_rows(eye_ref, x):
    return _dot_nt(eye_ref[...], x).astype(BF16)


def _rope_rows(x, cos, sin, rot_ref):
    return x * cos + _dot(x.astype(BF16), rot_ref[...]) * sin


def _attn_specs(n_batch, seq, ctx_len, q_width, kv_width):
    nq = seq // TQ
    ctx0 = n_batch * seq // ctx_len

    def q_idx(b, i):
        return (jnp.where(i < nq, b * nq + i, ctx0 + b), 0)

    return dict(
        q=lambda w: pl.BlockSpec((TQ, w), q_idx),
        lat=lambda w: pl.BlockSpec((seq, w), lambda b, i: (b, 0)),
        ctx=lambda w: pl.BlockSpec((ctx_len, w), lambda b, i: (ctx0 + b, 0)),
        tab=pl.BlockSpec((TQ, HB), lambda b, i: (i, 0)),
        full=lambda shape: pl.BlockSpec(shape, lambda b, i: (0,) * len(shape)),
    )


def _gqa_kernel(q_ref, kvl_ref, kvc_ref, cosq_ref, sinq_ref, cosk_ref, sink_ref, qg_ref, kg_ref,
                ones_ref, rot_ref, eye_ref, o_ref, kl_s, kc_s, vtl_s, vtc_s, *, nq):
    i = pl.program_id(1)

    def head_rms(x):
        ss = _dot((x * x).astype(BF16), ones_ref[...])
        return lax.rsqrt(ss * (1.0 / HD) + EPS)

    @pl.when(i == 0)
    def _():
        kl = kvl_ref[:, 0:HB].astype(F32)
        kn = kl * head_rms(kl) * kg_ref[...]
        kl_s[...] = _rope_rows(kn, cosk_ref[...], sink_ref[...], rot_ref).astype(BF16)
        kc = kvc_ref[:, 0:HB].astype(F32)
        kc_s[...] = (kc * head_rms(kc) * kg_ref[...]).astype(BF16)
        vtl_s[...] = _transpose_rows(eye_ref, kvl_ref[:, HB:2 * HB])
        vtc_s[...] = _transpose_rows(eye_ref, kvc_ref[:, HB:2 * HB])

    seq, ctx_len = kl_s.shape[0], kc_s.shape[0]

    def run(with_latent):
        keys = (_row_chunks(kl_s, seq) if with_latent else []) + _row_chunks(kc_s, ctx_len)
        tasks = []
        for blk in range(A_HEADS // 2):
            q = q_ref[:, blk * HB:(blk + 1) * HB].astype(F32)
            qn = q * head_rms(q) * qg_ref[...]
            qr = _rope_rows(qn, cosq_ref[...], sinq_ref[...], rot_ref) * (HD ** -0.5 * LOG2E)
            vts = (_col_chunks(vtl_s, seq, slice(None)) if with_latent else []) \
                + _col_chunks(vtc_s, ctx_len, slice(None))
            tasks.append((_pair_query(qr), keys, vts, None))
        outs = _attend_heads(tasks)
        for blk in range(A_HEADS // 2):
            o_ref[:, blk * HB:(blk + 1) * HB] = _pair_rows(outs[blk]).astype(BF16)

    @pl.when(i < nq)
    def _():
        run(True)

    @pl.when(i == nq)
    def _():
        run(False)


def _gqa(qa, kva, tabs, q_gain, k_gain, consts, seq, ctx_len, n_batch, with_ctx):
    nq = seq // TQ
    sp = _attn_specs(n_batch, seq, ctx_len, YW, 2 * HB)
    n_out = n_batch * (seq + ctx_len) if with_ctx else n_batch * seq
    return pl.pallas_call(
        functools.partial(_gqa_kernel, nq=nq),
        out_shape=jax.ShapeDtypeStruct((n_out, YW), BF16),
        grid=(n_batch, nq + int(with_ctx)),
        in_specs=[
            sp["q"](YW), sp["lat"](2 * HB), sp["ctx"](2 * HB), sp["tab"], sp["tab"],
            sp["full"]((seq, HB)), sp["full"]((seq, HB)),
            sp["full"]((1, HB)), sp["full"]((1, HB)),
            sp["full"]((HB, HB)), sp["full"]((HB, HB)), sp["full"]((HB, HB)),
        ],
        out_specs=sp["q"](YW),
        scratch_shapes=[pltpu.VMEM((seq, HB), BF16), pltpu.VMEM((ctx_len, HB), BF16),
                        pltpu.VMEM((HB, seq), BF16), pltpu.VMEM((HB, ctx_len), BF16)],
        compiler_params=_params(("arbitrary", "arbitrary")),
        name="gqa",
    )(qa, kva, kva, tabs["cos_a"], tabs["sin_a"], tabs["cos_a"], tabs["sin_a"], q_gain, k_gain,
      consts["ones_a"], consts["rot_a"], consts["eye"])


NA_TILE_ROWS = TQ // GRID_W
NA_BAND = NA_R + NA_TILE_ROWS
NA_KEYS = NA_BAND * GRID_W


def _na_band_start(i, rows):
    return jnp.clip(i * NA_TILE_ROWS - NA_R // 2, 0, rows - NA_BAND)


def _na_kernel(q_ref, kl_ref, vl_ref, kc_ref, vc_ref, bias_ref, eye_ref, o_ref, vtl_s, vtc_s, *, rows, nq):
    i = pl.program_id(1)
    scale = HD ** -0.5 * LOG2E
    band_tiles = NA_KEYS // TQ

    @pl.when(i == 0)
    def _():
        for blk in range(3):
            lanes = slice(blk * HB, (blk + 1) * HB)
            for t in range(nq):
                vtl_s[t, lanes, :] = _transpose_rows(eye_ref, vl_ref[t * TQ:(t + 1) * TQ, lanes])
            vtc_s[lanes, :] = _transpose_rows(eye_ref, vc_ref[:, lanes])

    def run(t0):
        tasks = []
        for blk in range(3):
            paired = 2 * blk + 1 < B_HEADS
            lanes = slice(blk * HB, (blk + 1) * HB)
            hrows = lanes if paired else slice(blk * HB, blk * HB + HD)
            q = q_ref[:, lanes].astype(F32) * scale
            keys = [lambda lanes=lanes: kc_ref[:, lanes]]
            vts = [lambda hrows=hrows: vtc_s[hrows, :]]
            biases = None
            if t0 is not None:
                keys = [lambda j=j, lanes=lanes: kl_ref[pl.ds(pl.multiple_of((t0 + j) * TQ, TQ), TQ), lanes]
                        for j in range(band_tiles)] + keys
                vts = [lambda j=j, hrows=hrows: vtl_s[t0 + j, hrows, :] for j in range(band_tiles)] + vts

                def bias_chunk(j, blk=blk, paired=paired):
                    rows = slice(j * TQ, (j + 1) * TQ)
                    if not paired:
                        return bias_ref[2 * blk, rows, :]
                    return jnp.concatenate([bias_ref[2 * blk, rows, :], bias_ref[2 * blk + 1, rows, :]], 1)

                biases = [functools.partial(bias_chunk, j) for j in range(band_tiles)] + [None]
            tasks.append((_pair_query(q) if paired else _half_query(q, 0), keys, vts, biases))
        outs = _attend_heads(tasks)
        for blk in range(3):
            o = outs[blk]
            y = _pair_rows(o) if o.shape[1] == 2 * TQ else _head_pair_rows(o, jnp.zeros_like(o))
            o_ref[:, blk * HB:(blk + 1) * HB] = y.astype(BF16)

    @pl.when(i < nq)
    def _():
        run(_na_band_start(i, rows) // NA_TILE_ROWS)

    @pl.when(i == nq)
    def _():
        run(None)


def _na_tile_config(i, nq):
    return jnp.where(i == 0, 0, jnp.where(i >= nq - 1, 2, 1))


def _na(qb, kb, vb, bias, eye, seq, ctx_len, n_batch, with_ctx):
    nq = seq // TQ
    sp = _attn_specs(n_batch, seq, ctx_len, YW, YW)
    n_out = n_batch * (seq + ctx_len) if with_ctx else n_batch * seq
    bias_spec = pl.BlockSpec((None,) + bias.shape[1:], lambda b, i: (_na_tile_config(i, nq), 0, 0, 0))
    return pl.pallas_call(
        functools.partial(_na_kernel, rows=seq // GRID_W, nq=nq),
        out_shape=jax.ShapeDtypeStruct((n_out, YW), BF16),
        grid=(n_batch, nq + int(with_ctx)),
        in_specs=[sp["q"](YW), sp["lat"](YW), sp["lat"](YW), sp["ctx"](YW), sp["ctx"](YW), bias_spec,
                  sp["full"]((HB, HB))],
        out_specs=sp["q"](YW),
        scratch_shapes=[pltpu.VMEM((nq, YW, TQ), BF16), pltpu.VMEM((YW, ctx_len), BF16)],
        compiler_params=_params(("arbitrary", "arbitrary")),
        name="na",
    )(qb, kb, vb, kb, vb, bias, eye)


def _na_window_pattern(i, rows):
    r = i * NA_TILE_ROWS + np.arange(NA_TILE_ROWS)
    r0 = np.clip(r - NA_R // 2, 0, rows - NA_R)
    krow = int(np.clip(i * NA_TILE_ROWS - NA_R // 2, 0, rows - NA_BAND)) + np.arange(NA_BAND)
    valid = (krow[None, :] >= r0[:, None]) & (krow[None, :] < r0[:, None] + NA_R)
    dr = np.clip(krow[None, :] - r[:, None] + (NA_R - 1), 0, 2 * NA_R - 2)
    return valid, dr


def _na_bias(rpb, rows, nq):
    col = jnp.arange(GRID_W, dtype=jnp.int32)
    c0 = jnp.clip(col - NA_C // 2, 0, GRID_W - NA_C)
    col_in = (col[None, :] >= c0[:, None]) & (col[None, :] < c0[:, None] + NA_C)
    dc = jnp.clip(col[None, :] - col[:, None] + (NA_C - 1), 0, 2 * NA_C - 2)
    tbl = jnp.where(col_in[None, None], rpb[:, :, dc] * LOG2E, NEG_INF)
    patterns = [_na_window_pattern(i, rows) for i in range(nq)]
    for i in range(2, nq - 1):
        assert all(np.array_equal(a, b) for a, b in zip(patterns[i], patterns[1]))
    out = []
    for valid, dr in (patterns[0], patterns[1], patterns[nq - 1]):
        band = jnp.where(jnp.asarray(valid)[None, :, :, None, None], tbl[:, jnp.asarray(dr)], NEG_INF)
        out.append(band.transpose(0, 2, 4, 1, 3).reshape(B_HEADS, NA_KEYS, TQ))
    return jnp.stack(out).astype(F32)


C_QW = C_HEADS * HB


def _mla_kernel(cq_ref, kvl_ref, kvc_ref, cosq_ref, sinq_ref, cosk_ref, sink_ref, qg_ref, kg_ref,
                wuq_ref, wuk_ref, wuvt_ref, place_ref, rotk_ref, o_ref, kl_s, kc_s, vl_s, vc_s, *, nq):
    i = pl.program_id(1)

    def latent_rms(x, gain):
        ms = jnp.mean(x * x, -1, keepdims=True)
        return (x * lax.rsqrt(ms + EPS) * gain).astype(BF16)

    @pl.when(i == 0)
    def _():
        cl = latent_rms(kvl_ref[:, 0:C_KVR].astype(F32), kg_ref[...])
        rl = _rope_rows(kvl_ref[:, C_KVR:2 * C_KVR].astype(F32), cosk_ref[...], sink_ref[...], rotk_ref)
        kl_s[...] = (_dot(cl, wuk_ref[...]) + _dot(rl.astype(BF16), place_ref[...])).astype(BF16)
        vl_s[...] = _dot_nt(wuvt_ref[...], cl).astype(BF16)
        cc = latent_rms(kvc_ref[:, 0:C_KVR].astype(F32), kg_ref[...])
        kc_s[...] = (_dot(cc, wuk_ref[...]) + _dot(kvc_ref[:, C_KVR:2 * C_KVR], place_ref[...])).astype(BF16)
        vc_s[...] = _dot_nt(wuvt_ref[...], cc).astype(BF16)

    seq, ctx_len = kl_s.shape[0], kc_s.shape[0]

    def run(with_latent):
        cq = latent_rms(cq_ref[...].astype(F32), qg_ref[...])
        qq = _dot(cq, wuq_ref[...])
        tasks = []
        for h in range(C_HEADS):
            lanes = slice(h * HB, (h + 1) * HB)
            q, q_rot = qq[:, lanes], qq[:, C_QW + h * HB:C_QW + (h + 1) * HB]
            qr = (q * cosq_ref[...] + q_rot * sinq_ref[...]) * ((C_NOPE + C_ROPE) ** -0.5 * LOG2E)
            hrows = slice(h * C_V, (h + 1) * C_V)
            keys, vts = _row_chunks(kc_s, ctx_len, lanes), _col_chunks(vc_s, ctx_len, hrows)
            if with_latent:
                keys, vts = _row_chunks(kl_s, seq, lanes) + keys, _col_chunks(vl_s, seq, hrows) + vts
            tasks.append((lambda qr=qr: qr.astype(BF16), keys, vts, None))
        outs = _attend_heads(tasks)
        outs = [jnp.zeros_like(outs[0])] + outs
        for blk in range(3):
            o_ref[:, blk * HB:(blk + 1) * HB] = _head_pair_rows(outs[2 * blk], outs[2 * blk + 1]).astype(BF16)

    @pl.when(i < nq)
    def _():
        run(True)

    @pl.when(i == nq)
    def _():
        run(False)


def _mla(cq, ckv, tabs, q_gain, kv_gain, wts, consts, seq, ctx_len, n_batch, with_ctx):
    nq = seq // TQ
    sp = _attn_specs(n_batch, seq, ctx_len, C_QR, 2 * C_KVR)
    full = sp["full"]
    n_out = n_batch * (seq + ctx_len) if with_ctx else n_batch * seq
    return pl.pallas_call(
        functools.partial(_mla_kernel, nq=nq),
        out_shape=jax.ShapeDtypeStruct((n_out, YW), BF16),
        grid=(n_batch, nq + int(with_ctx)),
        in_specs=[
            sp["q"](C_QR), sp["lat"](2 * C_KVR), sp["ctx"](2 * C_KVR), sp["tab"], sp["tab"],
            full((seq, HB)), full((seq, HB)), full((1, C_QR)), full((1, C_KVR)),
            full((C_QR, 2 * C_QW)), full((C_KVR, C_QW)), full((C_HEADS * C_V, C_KVR)),
            full((HB, C_QW)), full((HB, HB)),
        ],
        out_specs=sp["q"](YW),
        scratch_shapes=[pltpu.VMEM((seq, C_QW), BF16), pltpu.VMEM((ctx_len, C_QW), BF16),
                        pltpu.VMEM((C_HEADS * C_V, seq), BF16), pltpu.VMEM((C_HEADS * C_V, ctx_len), BF16)],
        compiler_params=_params(("arbitrary", "arbitrary")),
        name="mla",
    )(cq, ckv, ckv, tabs["cos_cq"], tabs["sin_cq"], tabs["cos_ck"], tabs["sin_ck"], q_gain, kv_gain,
      wts["wuq"], wts["wuk"], wts["wuvt"], consts["place_c"], consts["rot_ck"])


ROUTER_ROWS = (N_GROUPS + 1) * EPG


def _outproj_kernel(*refs, n_src, n_lat_tiles):
    x_refs = refs[:n_src]
    (ya_ref, yb_ref, yc_ref, mod_ref, wa_ref, wb_ref, wc_ref, g_ref, b_ref, wrt_ref, brt_ref, tri_ref,
     x1_ref, h2_ref, idx_ref, cnt_ref, carry_s) = refs[n_src:]
    i = pl.program_id(0)

    @pl.when(i == 0)
    def _():
        carry_s[...] = jnp.zeros_like(carry_s)

    y = _dot(ya_ref[...], wa_ref[...]) + _dot(yb_ref[...], wb_ref[...]) + _dot(yc_ref[...], wc_ref[...])

    def residual(x_ref):
        x1_ref[...] = _ln_rows(ALPHA * x_ref[...] + mod_ref[2:3, :] * y) * g_ref[...] + b_ref[...]

    _for_row_source(x_refs, n_lat_tiles, residual)
    h2 = _ln_rows(x1_ref[...]) * (1.0 + mod_ref[4:5, :]) + mod_ref[3:4, :]
    h2_ref[:, 0:D] = h2

    logit = _dot_nt(wrt_ref[...], h2.astype(BF16)) + brt_ref[:, 0:1]
    sub = lax.broadcasted_iota(jnp.int32, (EPG, TM_OUT), 0)

    def col_max(v):
        return jnp.max(v, 0, keepdims=True)

    def first_row(mask):
        return jnp.min(jnp.where(mask, sub, EPG), 0, keepdims=True)

    g_logit = jnp.where(sub < N_GROUPS, logit[0:EPG], NEG_INF)
    g_max = col_max(g_logit)
    g_idx = first_row(g_logit == g_max)
    g_w = 1.0 / jnp.sum(jnp.exp(g_logit - g_max), 0, keepdims=True)
    e_logit = logit[EPG:2 * EPG]
    for g in range(1, N_GROUPS):
        e_logit = jnp.where(g_idx == g, logit[(g + 1) * EPG:(g + 2) * EPG], e_logit)
    m1 = col_max(e_logit)
    i1 = first_row(e_logit == m1)
    rest = jnp.where(sub == i1, NEG_INF, e_logit)
    m2 = col_max(rest)
    i2 = first_row(rest == m2)
    t = jnp.exp(m2 - m1)
    w1 = g_w / (1.0 + t)
    w2 = g_w * t / (1.0 + t)
    lo, hi = jnp.minimum(i1, i2), jnp.maximum(i1, i2)
    pair = lax.shift_right_logical(lo * (2 * EPG - 1 - lo), jnp.ones_like(lo)) + hi - lo - 1
    cls = g_idx * N_PAIR + pair
    w_lo = jnp.where(i1 < i2, w1, w2)
    w_hi = jnp.where(i1 < i2, w2, w1)

    cls_row = lax.broadcasted_iota(jnp.int32, (LANE, TM_OUT), 0)
    onehot = (cls_row == cls).astype(F32)
    prefix = _dot(onehot.astype(BF16), tri_ref[...]) + carry_s[:, 0:1]
    rank = jnp.sum(onehot * prefix, 0, keepdims=True)
    carry_s[...] = carry_s[...] + jnp.sum(onehot, 1, keepdims=True)
    cnt_ref[...] = carry_s[...]

    idx_ref[...] = jnp.where(sub == 0, cls, jnp.where(sub == 1, rank.astype(jnp.int32), 0))
    rows = jnp.where(sub == 0, w_lo, jnp.where(sub == 1, w_hi, 0.0))
    h2_ref[:, D:DX] = jnp.concatenate([rows, jnp.zeros((LANE - EPG, TM_OUT), F32)], 0).T


def _outproj(ya, yb, yc, xs, mod, w_parts, ln_g, ln_b, w_rt, b_rt, tri, n_rows, seq, n_lat, n_batch):
    row = lambda w: pl.BlockSpec((TM_OUT, w), lambda i: (i, 0))
    full = lambda shape: pl.BlockSpec(shape, lambda i: (0, 0))
    n_tiles = n_rows // TM_OUT
    return pl.pallas_call(
        functools.partial(_outproj_kernel, n_src=len(xs), n_lat_tiles=n_lat // TM_OUT),
        out_shape=[
            jax.ShapeDtypeStruct((n_rows, D), F32),
            jax.ShapeDtypeStruct((n_rows, DX), F32),
            jax.ShapeDtypeStruct((n_tiles, EPG, TM_OUT), jnp.int32),
            jax.ShapeDtypeStruct((LANE, LANE), F32),
        ],
        grid=(n_tiles,),
        in_specs=_row_source_specs(xs, TM_OUT, n_lat) + [
            row(YW), row(YW), row(YW),
            pl.BlockSpec((None, N_MOD, D), _mod_index(TM_OUT, seq, n_lat, n_batch)),
            full((YW, D)), full((YW, D)), full((YW, D)),
            full((1, D)), full((1, D)), full((ROUTER_ROWS, D)), full((ROUTER_ROWS, LANE)), full((TM_OUT, TM_OUT)),
        ],
        out_specs=[row(D), row(DX), pl.BlockSpec((None, EPG, TM_OUT), lambda i: (i, 0, 0)),
                   full((LANE, LANE))],
        scratch_shapes=[pltpu.VMEM((LANE, LANE), F32)],
        compiler_params=_params(("arbitrary",)),
        name="outproj",
    )(*xs, ya, yb, yc, mod, *w_parts, ln_g, ln_b, w_rt, b_rt, tri)


SUB = 8


def _wait_rows(make_copy):
    def wait(u, c):
        for k in range(SUB):
            make_copy(0, 0, 0).wait()
        return c

    lax.fori_loop(0, T_ROW // SUB, wait, 0)


def _permute_rows(starts_ref, idx_hbm, idx_s, sem_idx, make_copy, n=None, drain=True):
    i = pl.program_id(0)
    n = pl.num_programs(0) if n is None else n

    def index_copies(step, slot):
        return [pltpu.make_async_copy(idx_hbm.at[step, k], idx_s[2 * slot + k], sem_idx.at[slot, k])
                for k in range(2)]

    @pl.when(i == 0)
    def _():
        for cp in index_copies(0, 0):
            cp.start()

    for slot in range(2):

        @pl.when(lax.rem(i, 2) == slot)
        def _(slot=slot):
            for cp in index_copies(i, slot):
                cp.wait()

            @pl.when(i + 1 < n)
            def _():
                for cp in index_copies(i + 1, 1 - slot):
                    cp.start()

            cls_s, rank_s = idx_s[2 * slot], idx_s[2 * slot + 1]
            copy = make_copy(slot)

            def start(u, c):
                for k in range(SUB):
                    t = u * SUB + k
                    copy(u, k, starts_ref[cls_s[t]] + rank_s[t]).start(priority=k % 2)
                return c

            lax.fori_loop(0, T_ROW // SUB, start, 0)
            if drain:
                _wait_rows(copy)


def _dispatch_kernel(starts_ref, idx_hbm, h_ref, hs_in, hs_out, *scratch):
    del hs_in
    idx_s, sem_idx, sem_rows = scratch[:4], scratch[4], scratch[5]
    _permute_rows(starts_ref, idx_hbm, idx_s, sem_idx,
                  lambda slot: lambda u, k, pos: pltpu.make_async_copy(
                      h_ref.at[u, pl.ds(k, 1)], hs_out.at[pl.ds(pos, 1)], sem_rows))


_ROW_IDX_SCRATCH = [pltpu.SMEM((T_ROW,), jnp.int32)] * 4 + [pltpu.SemaphoreType.DMA((2, 2))]


def _dispatch(starts, idx, h2, hs0):
    any_spec = pl.BlockSpec(memory_space=pl.ANY)
    return pl.pallas_call(
        _dispatch_kernel,
        out_shape=jax.ShapeDtypeStruct(hs0.shape, hs0.dtype),
        grid_spec=pltpu.PrefetchScalarGridSpec(
            num_scalar_prefetch=1,
            grid=(idx.shape[0],),
            in_specs=[any_spec, pl.BlockSpec((T_ROW // SUB, SUB, DX), lambda i, st: (i, 0, 0)), any_spec],
            out_specs=any_spec,
            scratch_shapes=_ROW_IDX_SCRATCH + [pltpu.SemaphoreType.DMA],
        ),
        input_output_aliases={3: 0},
        compiler_params=_params(("arbitrary",)),
        name="dispatch",
    )(starts, idx, h2.reshape(-1, SUB, DX), hs0)


def _moe_kernel(elo_ref, ehi_ref, valid_ref, h_ref, w1a, w3a, w2a, w1b, w3b, w2b, o_ref):
    j = pl.program_id(0)

    def ffn(h, w1, w3, w2):
        hid = (_silu(_dot(h, w1[...])) * _dot(h, w3[...])).astype(BF16)
        return _dot(hid, w2[...])

    @pl.when(valid_ref[j] != 0)
    def _():
        h = h_ref[:, 0:D].astype(BF16)
        w_lo, w_hi = h_ref[:, D:D + 1], h_ref[:, D + 1:D + 2]
        o_ref[...] = w_lo * ffn(h, w1a, w3a, w2a) + w_hi * ffn(h, w1b, w3b, w2b)

    @pl.when(valid_ref[j] == 0)
    def _():
        o_ref[...] = jnp.zeros_like(o_ref)


def _moe(tile_elo, tile_ehi, tile_valid, hs, w1, w3, w2):
    lo_spec = lambda shape: pl.BlockSpec((None,) + shape, lambda j, elo, ehi, v: (elo[j], 0, 0))
    hi_spec = lambda shape: pl.BlockSpec((None,) + shape, lambda j, elo, ehi, v: (ehi[j], 0, 0))
    up, down = (D, D_EXP), (D_EXP, D)
    return pl.pallas_call(
        _moe_kernel,
        out_shape=jax.ShapeDtypeStruct((hs.shape[0], D), F32),
        grid_spec=pltpu.PrefetchScalarGridSpec(
            num_scalar_prefetch=3,
            grid=(hs.shape[0] // TM_MOE,),
            in_specs=[
                pl.BlockSpec((TM_MOE, DX), lambda j, elo, ehi, v: (j, 0)),
                lo_spec(up), lo_spec(up), lo_spec(down), hi_spec(up), hi_spec(up), hi_spec(down),
            ],
            out_specs=pl.BlockSpec((TM_MOE, D), lambda j, elo, ehi, v: (j, 0)),
        ),
        compiler_params=_params(("arbitrary",)),
        name="moe_ffn",
    )(tile_elo, tile_ehi, tile_valid, hs, w1, w3, w2, w1, w3, w2)


def _combine_kernel(starts_ref, idx_hbm, ys_hbm, x1_ref, mod_ref, g_ref, b_ref, o_ref, *scratch):
    idx_s, sem_idx, buf, sem_rows = scratch[:4], scratch[4], scratch[5], scratch[6]
    i = pl.program_id(0)
    n_tiles = pl.num_programs(0) - 1

    def gather(slot):
        return lambda u, k, pos: pltpu.make_async_copy(ys_hbm.at[pl.ds(pos, 1)], buf.at[slot, u, pl.ds(k, 1)],
                                                       sem_rows.at[slot])

    @pl.when(i < n_tiles)
    def _():
        _permute_rows(starts_ref, idx_hbm, idx_s, sem_idx, gather, n=n_tiles, drain=False)

    for slot in range(2):

        @pl.when((i > 0) & (lax.rem(i + 1, 2) == slot))
        def _(slot=slot):
            _wait_rows(gather(slot))
            moe = buf[slot].reshape(T_ROW, D)
            o_ref[...] = _ln_rows(ALPHA * x1_ref[...] + mod_ref[5:6, :] * moe) * g_ref[...] + b_ref[...]


def _combine(starts, idx, ys, x1, mod, ln_g, ln_b, seq, n_lat, n_batch):
    n_rows = x1.shape[0]
    any_spec = pl.BlockSpec(memory_space=pl.ANY)
    mod_idx = _mod_index(T_ROW, seq, n_lat, n_batch)
    tile = lambda i: jnp.maximum(i - 1, 0)
    return pl.pallas_call(
        _combine_kernel,
        out_shape=jax.ShapeDtypeStruct((n_rows, D), F32),
        grid_spec=pltpu.PrefetchScalarGridSpec(
            num_scalar_prefetch=1,
            grid=(n_rows // T_ROW + 1,),
            in_specs=[
                any_spec, any_spec,
                pl.BlockSpec((T_ROW, D), lambda i, st: (tile(i), 0)),
                pl.BlockSpec((None, N_MOD, D), lambda i, st: mod_idx(tile(i))),
                pl.BlockSpec((1, D), lambda i, st: (0, 0)), pl.BlockSpec((1, D), lambda i, st: (0, 0)),
            ],
            out_specs=pl.BlockSpec((T_ROW, D), lambda i, st: (tile(i), 0)),
            scratch_shapes=_ROW_IDX_SCRATCH + [pltpu.VMEM((2, T_ROW // SUB, SUB, D), F32),
                                               pltpu.SemaphoreType.DMA((2,))],
        ),
        compiler_params=_params(("arbitrary",)),
        name="combine",
    )(starts, idx, ys, x1, mod, ln_g, ln_b)


def _rot_matrix(width, start, half):
    r = np.zeros((width, width), np.float32)
    for j in range(half):
        r[start + half + j, start + j] = -1.0
        r[start + j, start + half + j] = 1.0
    return r


def _constants():
    ones_a = np.kron(np.eye(2, dtype=np.float32), np.ones((HD, HD), np.float32))
    rot_a = _rot_matrix(HB, 0, HD // 2) + _rot_matrix(HB, HD, HD // 2)
    rot_ck = _rot_matrix(HB, 0, C_ROPE // 2)
    place = np.zeros((HB, C_QW), np.float32)
    for h in range(C_HEADS):
        for j in range(C_ROPE):
            place[j, h * HB + C_NOPE + j] = 1.0
    tri = np.triu(np.ones((TM_OUT, TM_OUT), np.float32), 1)
    cls_lo, cls_hi = [], []
    for g in range(N_GROUPS):
        for lo in range(EPG):
            for hi in range(lo + 1, EPG):
                cls_lo.append(g * EPG + lo)
                cls_hi.append(g * EPG + hi)
    as_bf = lambda a: jnp.asarray(a, BF16)
    return dict(ones_a=as_bf(ones_a), rot_a=as_bf(rot_a), rot_ck=as_bf(rot_ck),
                place_c=as_bf(place), tri=as_bf(tri), eye=as_bf(np.eye(HB, dtype=np.float32)),
                cls_lo=jnp.asarray(cls_lo, jnp.int32), cls_hi=jnp.asarray(cls_hi, jnp.int32))


def _rope_tables(seq):
    t = jnp.arange(seq, dtype=jnp.int32)
    row = (t // GRID_W).astype(F32)
    col = (t % GRID_W).astype(F32)

    def angles(dim):
        n_freq = dim // 4
        inv = THETA ** (-jnp.arange(n_freq, dtype=F32) / n_freq)
        ang = jnp.concatenate([row[:, None] * inv, col[:, None] * inv], -1)
        return jnp.concatenate([ang, jnp.zeros((TQ, dim // 2), F32)], 0)

    ang_a, ang_c = angles(HD), angles(C_ROPE)
    ones = lambda w: jnp.ones((seq + TQ, w), F32)
    zeros = lambda w: jnp.zeros((seq + TQ, w), F32)
    ca, sa = jnp.cos(ang_a), jnp.sin(ang_a)
    cc, sc = jnp.cos(ang_c), jnp.sin(ang_c)
    return dict(
        cos_a=jnp.tile(ca, (1, 4)), sin_a=jnp.tile(sa, (1, 4)),
        cos_cq=jnp.concatenate([ones(C_NOPE), cc, cc, ones(HB - C_NOPE - C_ROPE)], -1),
        sin_cq=jnp.concatenate([zeros(C_NOPE), sc, sc, zeros(HB - C_NOPE - C_ROPE)], -1),
        cos_ck=jnp.concatenate([cc, cc, ones(HB - C_ROPE)], -1),
        sin_ck=jnp.concatenate([sc, sc, zeros(HB - C_ROPE)], -1),
    )


def _pad_cols(w, width):
    return jnp.pad(w, ((0, 0), (0, width - w.shape[1])))


def _layer_weights(w_in, w_uq, w_ukv, w_out, w_rg, b_rg, w_re, b_re):
    pa = (A_HEADS + 2 * A_KV) * HD
    pb = 3 * B_HEADS * HD
    bw = B_HEADS * HD
    g = A_HEADS // A_KV
    qa = w_in[:, :A_HEADS * HD].reshape(D, A_KV, g, HD).transpose(0, 2, 1, 3).reshape(D, A_HEADS * HD)
    kva = w_in[:, A_HEADS * HD:pa]
    qb, kb, vb = (_pad_cols(w_in[:, pa + k * bw:pa + (k + 1) * bw], YW) for k in range(3))
    cq = w_in[:, pa + pb:pa + pb + C_QR]
    ckv = _pad_cols(w_in[:, pa + pb + C_QR:], 2 * C_KVR)
    w_all = jnp.concatenate([qa, kva, qb, kb, vb, cq, ckv], -1).astype(BF16)

    wuq = jnp.pad(w_uq.reshape(C_QR, C_HEADS, C_NOPE + C_ROPE), ((0, 0), (0, 0), (0, HB - C_NOPE - C_ROPE)))
    x1, x2 = wuq[:, :, C_NOPE:C_NOPE + C_ROPE // 2], wuq[:, :, C_NOPE + C_ROPE // 2:C_NOPE + C_ROPE]
    wuq_rot = jnp.concatenate([jnp.zeros_like(wuq[:, :, :C_NOPE]), -x2, x1,
                               jnp.zeros_like(wuq[:, :, C_NOPE + C_ROPE:])], -1)
    wuq = jnp.concatenate([wuq.reshape(C_QR, C_QW), wuq_rot.reshape(C_QR, C_QW)], -1).astype(BF16)
    ukv = w_ukv.reshape(C_KVR, C_HEADS, C_NOPE + C_V)
    wuk = jnp.pad(ukv[:, :, :C_NOPE], ((0, 0), (0, 0), (0, HB - C_NOPE))).reshape(C_KVR, C_QW).astype(BF16)
    wuvt = ukv[:, :, C_NOPE:].reshape(C_KVR, C_HEADS * C_V).T.astype(BF16)

    oa = w_out[:A_HEADS * HD].reshape(A_KV, g, HD, D).transpose(1, 0, 2, 3).reshape(A_HEADS * HD, D)
    ob = jnp.pad(w_out[A_HEADS * HD:A_HEADS * HD + bw], ((0, YW - bw), (0, 0)))
    oc = jnp.pad(w_out[A_HEADS * HD + bw:], ((C_V, 0), (0, 0)))
    w_parts = tuple(w.astype(BF16) for w in (oa, ob, oc))

    w_rt = jnp.zeros((ROUTER_ROWS, D), F32).at[:N_GROUPS].set(w_rg.T).at[EPG:].set(w_re.T).astype(BF16)
    b_rt = jnp.zeros((ROUTER_ROWS,), F32).at[:N_GROUPS].set(b_rg).at[EPG:].set(b_re)
    b_rt = jnp.broadcast_to(b_rt[:, None], (ROUTER_ROWS, LANE))
    return w_all, dict(wuq=wuq, wuk=wuk, wuvt=wuvt), w_parts, w_rt, b_rt


def kernel(x, c, ctx, c_ctx, w_mod, b_mod, w_in, q_gain_a, k_gain_a, rpb_b, q_lat_gain, kv_lat_gain,
           w_uq, w_ukv, w_out, ln1_g, ln1_b, w_rg, b_rg, w_re, b_re, w1, w3, w2, ln2_g, ln2_b):
    n_batch, seq, _ = x.shape
    ctx_len = ctx.shape[1]
    n_lat, n_ctx = n_batch * seq, n_batch * ctx_len
    assert ctx_len == TQ and seq % TM_IN == 0 and n_ctx % T_ROW == 0 and seq // GRID_W >= NA_BAND
    assert T_ROW == TM_OUT

    consts = _constants()
    tabs = _rope_tables(seq)
    mod_rows = -(-(n_batch + 1) // 8) * 8
    cc = jnp.zeros((mod_rows, D), F32).at[:n_batch].set(c).at[n_batch].set(c_ctx)
    xs = (x.reshape(n_lat, D), ctx.reshape(n_ctx, D))

    for l in range(DEPTH):
        with_ctx = l < DEPTH - 1
        mod = _modulation(cc, w_mod[l], b_mod[l]).reshape(mod_rows, N_MOD, D)
        w_all, mla_w, w_parts, w_rt, b_rt = _layer_weights(w_in[l], w_uq[l], w_ukv[l], w_out[l],
                                                          w_rg[l], b_rg[l], w_re[l], b_re[l])
        qa, kva, qb, kb, vb, cq, ckv = _inproj(xs, mod, w_all, seq, n_lat, n_batch)
        ya = _gqa(qa, kva, tabs, jnp.tile(q_gain_a[l], 2)[None], jnp.tile(k_gain_a[l], 2)[None],
                  consts, seq, ctx_len, n_batch, with_ctx)
        yb = _na(qb, kb, vb, _na_bias(rpb_b[l], seq // GRID_W, seq // TQ), consts["eye"],
                 seq, ctx_len, n_batch, with_ctx)
        yc = _mla(cq, ckv, tabs, q_lat_gain[l][None], kv_lat_gain[l][None], mla_w, consts,
                  seq, ctx_len, n_batch, with_ctx)

        n_rows = n_lat + n_ctx if with_ctx else n_lat
        x1, h2, idx, counts = _outproj(ya, yb, yc, xs, mod, w_parts, ln1_g[l][None], ln1_b[l][None],
                                       w_rt, b_rt, consts["tri"], n_rows, seq, n_lat, n_batch)

        cnt = counts[:N_CLS, 0].astype(jnp.int32)
        padded = (cnt + TM_MOE - 1) // TM_MOE * TM_MOE
        ends = jnp.cumsum(padded)
        starts = ends - padded
        n_tiles = n_rows // TM_MOE + N_CLS
        tile_start = jnp.arange(n_tiles, dtype=jnp.int32) * TM_MOE
        tile_cls = jnp.sum((ends[None, :] <= tile_start[:, None]).astype(jnp.int32), -1)
        tile_cls = jnp.minimum(tile_cls, N_CLS - 1)
        tile_valid = (tile_start < ends[-1]).astype(jnp.int32)

        hs = _dispatch(starts, idx, h2, jnp.zeros((n_tiles * TM_MOE, DX), F32))
        in_cls = tile_cls[:, None] == jnp.arange(N_CLS, dtype=jnp.int32)[None, :]
        tile_elo, tile_ehi = (jnp.sum(jnp.where(in_cls, consts[k][None, :], 0), -1) for k in ("cls_lo", "cls_hi"))
        ys = _moe(tile_elo, tile_ehi, tile_valid, hs,
                  w1[l].astype(BF16), w3[l].astype(BF16), w2[l].astype(BF16))
        xs = (_combine(starts, idx, ys, x1, mod, ln2_g[l][None], ln2_b[l][None], seq, n_lat, n_batch),)

    return xs[0].reshape(n_batch, seq, D)
```
